```python
import math
import jax, jax.numpy as jnp
from jax import lax
import numpy as np

D_MODEL = 1024
BATCH = 8
SEQ = 4096
DEPTH = 4

N_META = 16
CHUNK = 64
META_PAD = CHUNK - N_META
CONV_K = 4
D_FF = 2816
LRU_WIDTH = D_MODEL // 4
LRU_HEADS = 4
LRU_BLOCK = LRU_WIDTH // LRU_HEADS
LRU_C = 8.0
SSD_HEADS = 8
SSD_HEADDIM = 64
SSD_INNER = SSD_HEADS * SSD_HEADDIM
SSD_GROUPS = 2
SSD_HPG = SSD_HEADS // SSD_GROUPS
SSD_STATE = 64
SSD_CONV_DIM = SSD_INNER + 2 * SSD_GROUPS * SSD_STATE
DN_HEADS = 4
DN_DK = 64
DN_DV = 64
DN_CONV_DIM = DN_HEADS * (2 * DN_DK + DN_DV)
MIX_WIDTH = LRU_WIDTH + SSD_INNER + DN_HEADS * DN_DV
IN_SIZES = (LRU_WIDTH, LRU_WIDTH,
            SSD_INNER, SSD_CONV_DIM, SSD_HEADS,
            DN_CONV_DIM, DN_HEADS * DN_DV, DN_HEADS, DN_HEADS)
D_IN = sum(IN_SIZES)
IN_OFFSETS = [int(s) for s in np.cumsum(IN_SIZES)[:-1]]
DEEPNORM_ALPHA = (2 * DEPTH) ** 0.25
DEEPNORM_BETA = (8 * DEPTH) ** -0.25
FFN_RES = 0.5
LN_EPS = 1e-5
RMS_EPS = 1e-6

kernel_name = 'hymba_deepnorm_lru_ssd_deltanet_macaron'

F32 = jnp.float32


def _layer_norm(x, g, b):
    xf = x.astype(F32)
    mu = jnp.mean(xf, axis=-1, keepdims=True)
    var = jnp.mean(jnp.square(xf - mu), axis=-1, keepdims=True)
    return ((xf - mu) * lax.rsqrt(var + LN_EPS) * g.astype(F32) + b.astype(F32)).astype(x.dtype)


def _rms_norm(x, w):
    xf = x.astype(F32)
    return xf * lax.rsqrt(jnp.mean(xf * xf, axis=-1, keepdims=True) + RMS_EPS) * w.astype(F32)


def _l2norm(x):
    return x * lax.rsqrt(jnp.sum(x * x, axis=-1, keepdims=True) + RMS_EPS)


def _causal_conv(x, w, b=None):
    c = x.shape[-1]
    y = lax.conv_general_dilated(x, w[:, None, :].astype(x.dtype), window_strides=(1,),
                                 padding=[(CONV_K - 1, 0)], dimension_numbers=('NWC', 'WIO', 'NWC'),
                                 feature_group_count=c)
    if b is not None:
        y = y + b.astype(x.dtype)
    return y


def _pad_front(x):
    pad = [(0, 0)] * x.ndim
    pad[1] = (META_PAD, 0)
    return jnp.pad(x, pad)


def _swiglu(x, wg, wu, wd):
    return (jax.nn.silu(x @ wg) * (x @ wu)) @ wd


def _segsum(x):
    cs = jnp.cumsum(x, axis=-1)
    n = x.shape[-1]
    mask = jnp.tril(jnp.ones((n, n), dtype=bool))
    return jnp.where(mask, cs[..., :, None] - cs[..., None, :], -jnp.inf)


def _linear_combine(c1, c2):
    a1, b1 = c1
    a2, b2 = c2
    return a1 * a2, a2 * b1 + b2


def _rg_lru_group(u_raw, y_raw, conv_w, conv_b, w_a, b_a, w_x, b_x, lam):
    u = _causal_conv(u_raw, conv_w, conv_b).astype(F32)
    bsz, t, _ = u.shape
    uh = u.reshape(bsz, t, LRU_HEADS, LRU_BLOCK)
    r = jax.nn.sigmoid(jnp.einsum('bthi,hij->bthj', uh, w_a.astype(F32)).reshape(bsz, t, LRU_WIDTH) + b_a.astype(F32))
    i = jax.nn.sigmoid(jnp.einsum('bthi,hij->bthj', uh, w_x.astype(F32)).reshape(bsz, t, LRU_WIDTH) + b_x.astype(F32))
    log_a = -LRU_C * r * jax.nn.softplus(-lam.astype(F32))
    a = jnp.exp(log_a)
    b = jnp.sqrt(-jnp.expm1(2.0 * log_a)) * (i * u)
    _, h = lax.associative_scan(_linear_combine, (a, b), axis=1)
    return (h * jax.nn.gelu(y_raw.astype(F32))).astype(u_raw.dtype)


def _ssd_chunked(x, a, b, c):
    bsz, L, ng, ne, p = x.shape
    nc = L // CHUNK
    x = x.reshape(bsz, nc, CHUNK, ng, ne, p)
    b = b.reshape(bsz, nc, CHUNK, ng, -1)
    c = c.reshape(bsz, nc, CHUNK, ng, -1)
    a = jnp.transpose(a.reshape(bsz, nc, CHUNK, ng, ne), (0, 3, 4, 1, 2))
    a_cs = jnp.cumsum(a, axis=-1)
    l_mat = jnp.exp(_segsum(a))
    y_diag = jnp.einsum('bclgn,bcsgn,bgecls,bcsgep->bclgep', c, b, l_mat, x)
    decay_states = jnp.exp(a_cs[..., -1:] - a_cs)
    states = jnp.einsum('bclgn,bgecl,bclgep->bcgepn', b, decay_states, x)
    states = jnp.concatenate([jnp.zeros_like(states[:, :1]), states], axis=1)
    chunk_tot = jnp.pad(a_cs[..., -1], ((0, 0), (0, 0), (0, 0), (1, 0)))
    decay_chunk = jnp.exp(_segsum(chunk_tot))
    states = jnp.einsum('bgezc,bcgepn->bzgepn', decay_chunk, states)[:, :-1]
    y_off = jnp.einsum('bclgn,bcgepn,bgecl->bclgep', c, states, jnp.exp(a_cs))
    return (y_diag + y_off).reshape(bsz, L, ng, ne, p)


def _ssd_group(z, xbc_raw, dt_raw, conv_w, conv_b, dt_bias, a_log, d_skip, norm_w):
    xbc = jax.nn.silu(_causal_conv(xbc_raw, conv_w, conv_b)).astype(F32)
    bsz, t, _ = xbc.shape
    xs = xbc[..., :SSD_INNER].reshape(bsz, t, SSD_GROUPS, SSD_HPG, SSD_HEADDIM)
    bm = xbc[..., SSD_INNER:SSD_INNER + SSD_GROUPS * SSD_STATE].reshape(bsz, t, SSD_GROUPS, SSD_STATE)
    cm = xbc[..., SSD_INNER + SSD_GROUPS * SSD_STATE:].reshape(bsz, t, SSD_GROUPS, SSD_STATE)
    dt = jax.nn.softplus(dt_raw.astype(F32) + dt_bias.astype(F32)).reshape(bsz, t, SSD_GROUPS, SSD_HPG)
    a = -jnp.exp(a_log.astype(F32)).reshape(SSD_GROUPS, SSD_HPG)
    y = _ssd_chunked(_pad_front(xs * dt[..., None]), _pad_front(dt * a), _pad_front(bm), _pad_front(cm))[:, META_PAD:]
    y = y + d_skip.astype(F32).reshape(SSD_GROUPS, SSD_HPG)[:, :, None] * xs
    gw = SSD_HPG * SSD_HEADDIM
    y = y.reshape(bsz, t, SSD_GROUPS, gw) * jax.nn.silu(z.astype(F32)).reshape(bsz, t, SSD_GROUPS, gw)
    y = _rms_norm(y, norm_w.reshape(SSD_GROUPS, gw))
    return y.reshape(bsz, t, SSD_INNER).astype(z.dtype)


def _gated_delta_chunked(q, k, v, beta, g):
    bsz, nh, L, dk = q.shape
    dv = v.shape[-1]
    nc = L // CHUNK
    q = q.reshape(bsz, nh, nc, CHUNK, dk)
    k = k.reshape(bsz, nh, nc, CHUNK, dk)
    v = v.reshape(bsz, nh, nc, CHUNK, dv)
    beta = beta.reshape(bsz, nh, nc, CHUNK)
    g_cs = jnp.cumsum(g.reshape(bsz, nh, nc, CHUNK), axis=-1)
    incl = jnp.tril(jnp.ones((CHUNK, CHUNK), dtype=bool))
    strict = jnp.tril(jnp.ones((CHUNK, CHUNK), dtype=bool), k=-1)
    decay = jnp.exp(jnp.where(incl, g_cs[..., :, None] - g_cs[..., None, :], -jnp.inf))
    k_beta = k * beta[..., None]
    v_beta = v * beta[..., None]
    m = jnp.where(strict, jnp.einsum('bhnid,bhnjd->bhnij', k_beta, k) * decay, 0.0)
    eye = jnp.eye(CHUNK, dtype=m.dtype)
    rhs = jnp.concatenate([v_beta, k_beta * jnp.exp(g_cs)[..., None]], axis=-1)
    sol = lax.linalg.triangular_solve(eye + m, rhs, left_side=True, lower=True, unit_diagonal=True)
    u, w = sol[..., :dv], sol[..., dv:]
    attn = jnp.where(incl, jnp.einsum('bhnid,bhnjd->bhnij', q, k) * decay, 0.0)
    q_dec = q * jnp.exp(g_cs)[..., None]
    k_dec = k * jnp.exp(g_cs[..., -1:] - g_cs)[..., None]
    chunk_decay = jnp.exp(g_cs[..., -1])

    def step(state, inp):
        qd, kd, uu, ww, aa, cd = inp
        v_new = uu - jnp.einsum('bhcd,bhdv->bhcv', ww, state)
        out = jnp.einsum('bhcd,bhdv->bhcv', qd, state) + jnp.einsum('bhij,bhjv->bhiv', aa, v_new)
        state = state * cd[..., None, None] + jnp.einsum('bhcd,bhcv->bhdv', kd, v_new)
        return state, out

    xs = tuple(jnp.moveaxis(z, 2, 0) for z in (q_dec, k_dec, u, w, attn, chunk_decay))
    s0 = jnp.zeros((bsz, nh, dk, dv), q.dtype)
    _, o = lax.scan(step, s0, xs)
    return jnp.moveaxis(o, 0, 2).reshape(bsz, nh, L, dv)


def _deltanet_group(qkv_raw, gate_raw, beta_raw, alpha_raw, conv_w, a_log, dt_bias, norm_w):
    qkv = jax.nn.silu(_causal_conv(qkv_raw, conv_w)).astype(F32)
    bsz, t, _ = qkv.shape
    nq = DN_HEADS * DN_DK
    q = _l2norm(qkv[..., :nq].reshape(bsz, t, DN_HEADS, DN_DK)) * (DN_DK ** -0.5)
    k = _l2norm(qkv[..., nq:2 * nq].reshape(bsz, t, DN_HEADS, DN_DK))
    v = qkv[..., 2 * nq:].reshape(bsz, t, DN_HEADS, DN_DV)
    beta = jax.nn.sigmoid(beta_raw.astype(F32))
    g = -jnp.exp(a_log.astype(F32)) * jax.nn.softplus(alpha_raw.astype(F32) + dt_bias.astype(F32))
    to_h = lambda z: jnp.swapaxes(_pad_front(z), 1, 2)
    o = _gated_delta_chunked(to_h(q), to_h(k), to_h(v), to_h(beta), to_h(g))
    o = jnp.swapaxes(o, 1, 2)[:, META_PAD:]
    o = _rms_norm(o, norm_w) * jax.nn.silu(gate_raw.astype(F32).reshape(bsz, t, DN_HEADS, DN_DV))
    return o.reshape(bsz, t, DN_HEADS * DN_DV).astype(qkv_raw.dtype)


def setup_inputs(seed: int = 0) -> dict:
    key = jax.random.key(seed)
    ks = jax.random.split(key, 26)
    nrm = lambda k, shape, s: jax.random.normal(k, shape, F32) * s
    unif = lambda k, shape, lo, hi: jax.random.uniform(k, shape, F32, lo, hi)

    def dt_bias_init(k, n):
        dt = jnp.exp(unif(k, (DEPTH, n), math.log(1e-3), math.log(1e-1)))
        return dt + jnp.log(-jnp.expm1(-dt))

    a_c = unif(ks[14], (DEPTH, LRU_WIDTH), 0.9, 0.999)
    s = a_c ** (1.0 / LRU_C)
    return {
        'x': nrm(ks[0], (BATCH, SEQ, D_MODEL), 1.0),
        'meta': nrm(ks[1], (N_META, D_MODEL), 1.0),
        'ln_g': 1.0 + nrm(ks[2], (DEPTH, 3, D_MODEL), 0.02),
        'ln_b': nrm(ks[3], (DEPTH, 3, D_MODEL), 0.02),
        'ffn_w_gate': nrm(ks[4], (DEPTH, 2, D_MODEL, D_FF), D_MODEL ** -0.5),
        'ffn_w_up': nrm(ks[5], (DEPTH, 2, D_MODEL, D_FF), D_MODEL ** -0.5),
        'ffn_w_down': nrm(ks[6], (DEPTH, 2, D_FF, D_MODEL), D_FF ** -0.5 * DEEPNORM_BETA),
        'w_in': nrm(ks[7], (DEPTH, D_MODEL, D_IN), D_MODEL ** -0.5),
        'lru_conv_w': nrm(ks[8], (DEPTH, CONV_K, LRU_WIDTH), CONV_K ** -0.5),
        'lru_conv_b': nrm(ks[9], (DEPTH, LRU_WIDTH), 0.01),
        'lru_w_a': nrm(ks[10], (DEPTH, LRU_HEADS, LRU_BLOCK, LRU_BLOCK), LRU_BLOCK ** -0.5),
        'lru_b_a': nrm(ks[11], (DEPTH, LRU_WIDTH), 0.01),
        'lru_w_x': nrm(ks[12], (DEPTH, LRU_HEADS, LRU_BLOCK, LRU_BLOCK), LRU_BLOCK ** -0.5),
        'lru_b_x': nrm(ks[13], (DEPTH, LRU_WIDTH), 0.01),
        'lru_lambda': jnp.log(s) - jnp.log1p(-s),
        'ssd_conv_w': nrm(ks[15], (DEPTH, CONV_K, SSD_CONV_DIM), CONV_K ** -0.5),
        'ssd_conv_b': nrm(ks[16], (DEPTH, SSD_CONV_DIM), 0.01),
        'ssd_dt_bias': dt_bias_init(ks[17], SSD_HEADS),
        'ssd_a_log': jnp.log(unif(ks[18], (DEPTH, SSD_HEADS), 1.0, 16.0)),
        'ssd_d': 1.0 + nrm(ks[19], (DEPTH, SSD_HEADS), 0.01),
        'ssd_norm_w': 1.0 + nrm(ks[20], (DEPTH, SSD_INNER), 0.01),
        'dn_conv_w': nrm(ks[21], (DEPTH, CONV_K, DN_CONV_DIM), CONV_K ** -0.5),
        'dn_a_log': jnp.log(unif(ks[22], (DEPTH, DN_HEADS), 1.0, 16.0)),
        'dn_dt_bias': dt_bias_init(ks[23], DN_HEADS),
        'dn_norm_w': 1.0 + nrm(ks[24], (DEPTH, DN_DV), 0.01),
        'w_out': nrm(ks[25], (DEPTH, MIX_WIDTH, D_MODEL), MIX_WIDTH ** -0.5 * DEEPNORM_BETA),
    }


def reference(x, meta, ln_g, ln_b, ffn_w_gate, ffn_w_up, ffn_w_down, w_in,
              lru_conv_w, lru_conv_b, lru_w_a, lru_b_a, lru_w_x, lru_b_x, lru_lambda,
              ssd_conv_w, ssd_conv_b, ssd_dt_bias, ssd_a_log, ssd_d, ssd_norm_w,
              dn_conv_w, dn_a_log, dn_dt_bias, dn_norm_w, w_out):
    bsz = x.shape[0]
    h = jnp.concatenate([jnp.broadcast_to(meta.astype(x.dtype)[None], (bsz, N_META, D_MODEL)), x], axis=1)
    for l in range(DEPTH):
        h = _layer_norm(DEEPNORM_ALPHA * h + FFN_RES * _swiglu(h, ffn_w_gate[l, 0], ffn_w_up[l, 0], ffn_w_down[l, 0]),
                        ln_g[l, 0], ln_b[l, 0])
        (lru_u, lru_y, ssd_z, ssd_xbc, ssd_dt, dn_qkv, dn_gate, dn_beta, dn_alpha) = jnp.split(h @ w_in[l], IN_OFFSETS, axis=-1)
        o_lru = _rg_lru_group(lru_u, lru_y, lru_conv_w[l], lru_conv_b[l], lru_w_a[l], lru_b_a[l],
                              lru_w_x[l], lru_b_x[l], lru_lambda[l])
        o_ssd = _ssd_group(ssd_z, ssd_xbc, ssd_dt, ssd_conv_w[l], ssd_conv_b[l], ssd_dt_bias[l],
                           ssd_a_log[l], ssd_d[l], ssd_norm_w[l])
        o_dn = _deltanet_group(dn_qkv, dn_gate, dn_beta, dn_alpha, dn_conv_w[l], dn_a_log[l],
                               dn_dt_bias[l], dn_norm_w[l])
        mix = jnp.concatenate([o_lru, o_ssd, o_dn], axis=-1) @ w_out[l]
        h = _layer_norm(DEEPNORM_ALPHA * h + mix, ln_g[l, 1], ln_b[l, 1])
        h = _layer_norm(DEEPNORM_ALPHA * h + FFN_RES * _swiglu(h, ffn_w_gate[l, 1], ffn_w_up[l, 1], ffn_w_down[l, 1]),
                        ln_g[l, 2], ln_b[l, 2])
    return h[:, N_META:]
```

```python
import functools

import jax
import jax.numpy as jnp
from jax import lax
from jax.experimental import pallas as pl
from jax.experimental.pallas import tpu as pltpu

F32 = jnp.float32
BF16 = jnp.bfloat16

D_MODEL = 1024
N_META = 16
CONV_K = 4
D_FF = 2816
LRU_WIDTH = 256
LRU_HEADS = 4
LRU_BLOCK = 64
LRU_C = 8.0
SSD_HEADS = 8
SSD_HEADDIM = 64
SSD_INNER = 512
SSD_GROUPS = 2
SSD_HPG = 4
SSD_STATE = 64
SSD_CONV_DIM = 768
DN_HEADS = 4
DN_DK = 64
DN_DV = 64
DN_CONV_DIM = 768
FFN_RES = 0.5
LN_EPS = 1e-5
RMS_EPS = 1e-6

CONV_W = LRU_WIDTH + SSD_CONV_DIM + DN_CONV_DIM
COL_LRU_Y = CONV_W
COL_SSD_Z = COL_LRU_Y + LRU_WIDTH
COL_DN_GATE = COL_SSD_Z + SSD_INNER
COL_SMALL = COL_DN_GATE + DN_HEADS * DN_DV
LANES = 128
SUBLANES = 8
PROJ_W = COL_SMALL + LANES
SM_DT, SM_BETA, SM_ALPHA = 0, SSD_HEADS, SSD_HEADS + DN_HEADS
CHUNK = 64
V7X_VMEM_BYTES = 64 * 1024 * 1024


def _dot(a, b):
    return jnp.dot(a.astype(BF16), b.astype(BF16), preferred_element_type=F32)


def _dot_nt(a, b):
    return lax.dot_general(a.astype(BF16), b.astype(BF16), (((1,), (1,)), ((), ())), preferred_element_type=F32)


def _dot_tn(a, b):
    return lax.dot_general(a.astype(BF16), b.astype(BF16), (((0,), (0,)), ((), ())), preferred_element_type=F32)


def _segsum_lanes(x, ones_bd):
    hi = x.astype(BF16)
    lo = (x - hi.astype(F32)).astype(BF16)
    return (jnp.dot(hi, ones_bd, preferred_element_type=F32) + jnp.dot(lo, ones_bd, preferred_element_type=F32))


def _silu(x):
    return x * jax.nn.sigmoid(x)


def _softplus(x):
    return jnp.maximum(x, 0.0) + jnp.log1p(jnp.exp(-jnp.abs(x)))


def _gelu_tanh(x):
    return 0.5 * x * (1.0 + jnp.tanh(0.7978845608028654 * (x + 0.044715 * (x * x * x))))


def _layer_norm(y, g, b):
    mu = jnp.mean(y, axis=-1, keepdims=True)
    yc = y - mu
    var = jnp.mean(yc * yc, axis=-1, keepdims=True)
    return yc * lax.rsqrt(var + LN_EPS) * g + b


def _iota2(shape, dim):
    return lax.broadcasted_iota(jnp.int32, shape, dim)


def _tile_rows(x, n):
    return jnp.concatenate([x] * n, axis=0)


def _expand_cols(x, c0, n):
    rows = x.shape[0]
    lane = _iota2((rows, LANES), 1)
    outs = []
    for j in range(n // 2):
        a = jnp.broadcast_to(x[:, c0 + 2 * j:c0 + 2 * j + 1], (rows, LANES))
        b = jnp.broadcast_to(x[:, c0 + 2 * j + 1:c0 + 2 * j + 2], (rows, LANES))
        outs.append(jnp.where(lane < 64, a, b))
    return jnp.concatenate(outs, axis=1)


def _chunk_cumsum(x):
    pos = _iota2(x.shape, 0) & (CHUNK - 1)
    s = 1
    while s < CHUNK:
        x = x + jnp.where(pos >= s, pltpu.roll(x, s, 0), 0.0)
        s *= 2
    return x


def _ffn_ln_kernel(h_ref, wg_ref, wu_ref, wd_ref, lng_ref, lnb_ref, o_ref, act_ref, *, alpha, ff_chunk):
    h = h_ref[...]
    xb = h.astype(BF16)
    d_ff = wg_ref.shape[1]
    for c0 in range(0, d_ff, ff_chunk):
        g = jnp.dot(xb, wg_ref[:, c0:c0 + ff_chunk], preferred_element_type=F32)
        u = jnp.dot(xb, wu_ref[:, c0:c0 + ff_chunk], preferred_element_type=F32)
        act_ref[:, c0:c0 + ff_chunk] = (_silu(g) * u).astype(BF16)
    y = jnp.dot(act_ref[...], wd_ref[...], preferred_element_type=F32)
    o_ref[...] = _layer_norm(alpha * h + FFN_RES * y, lng_ref[...], lnb_ref[...])


def _resident(shape):
    return pl.BlockSpec(shape, lambda *_: (0,) * len(shape), pipeline_mode=pl.Buffered(1))


def _ffn_ln(h2d, wg, wu, wd, lng, lnb, *, alpha, tm):
    rows, d = h2d.shape
    d_ff = wg.shape[1]
    ff_chunk = 256
    weights = 3 * d * d_ff * 2
    tiles = 2 * 2 * tm * d * 4 + tm * d_ff * 2 + 4 * tm * ff_chunk * 4 + 2 * tm * d * 4
    vmem = min(weights + tiles + (8 << 20), V7X_VMEM_BYTES - (4 << 20))
    return pl.pallas_call(
        functools.partial(_ffn_ln_kernel, alpha=alpha, ff_chunk=ff_chunk),
        grid=(rows // tm,),
        in_specs=[
            pl.BlockSpec((tm, d), lambda i: (i, 0)),
            _resident((d, d_ff)), _resident((d, d_ff)), _resident((d_ff, d)),
            _resident((1, d)), _resident((1, d)),
        ],
        out_specs=pl.BlockSpec((tm, d), lambda i: (i, 0)),
        out_shape=jax.ShapeDtypeStruct((rows, d), F32),
        scratch_shapes=[pltpu.VMEM((tm, d_ff), BF16)],
        compiler_params=pltpu.CompilerParams(dimension_semantics=("arbitrary",), vmem_limit_bytes=vmem),
        name="ffn_ln",
    )(h2d, wg, wu, wd, lng, lnb)


def _mixer_kernel(h_ref, win_ref, cwb_ref, wgate_ref, gateb_ref, lam_ref, slabv_ref, ssdv_ref, dnv_ref,
                  onesbd_ref, wout_ref, lng_ref, lnb_ref, o_ref,
                  proj_scr, xe_scr, la_scr, lb_scr, q_scr, k_scr, v_scr, be_scr, cg_scr,
                  xs_scr, dte_scr, cse_scr, bm_scr, cm_scr, hl_scr, ys_scr, od_scr,
                  lruh_scr, ssds_scr, dns_scr, *, alpha, rows):
    tb = pl.program_id(1)

    @pl.when(tb == 0)
    def _():
        xe_scr[0:SUBLANES, :] = jnp.zeros((SUBLANES, CONV_W), F32)
        lruh_scr[...] = jnp.zeros_like(lruh_scr)
        ssds_scr[...] = jnp.zeros_like(ssds_scr)
        dns_scr[...] = jnp.zeros_like(dns_scr)

    h = h_ref[0]
    proj_scr[...] = jnp.dot(h.astype(BF16), win_ref[...], preferred_element_type=F32)

    xe_scr[SUBLANES:SUBLANES + rows, :] = proj_scr[:, 0:CONV_W]

    def conv_cols(c0, c1):
        xe = xe_scr[:, c0:c1]
        w = cwb_ref[:, c0:c1]
        y = xe[SUBLANES:] * w[3:4] + w[4:5]
        for s in (1, 2, 3):
            y = y + pltpu.roll(xe, s, 0)[SUBLANES:] * w[3 - s:4 - s]
        return y

    u = conv_cols(0, LRU_WIDTH)
    gates = _dot(u, wgate_ref[...]) + gateb_ref[...]
    r = jax.nn.sigmoid(gates[:, :LRU_WIDTH])
    i_gate = jax.nn.sigmoid(gates[:, LRU_WIDTH:])
    log_a = (-LRU_C) * r * _softplus(-lam_ref[...])
    a = jnp.exp(log_a)
    la_scr[...] = a
    lb_scr[...] = jnp.sqrt(1.0 - a * a) * (i_gate * u)

    xbc = _silu(conv_cols(LRU_WIDTH, LRU_WIDTH + SSD_CONV_DIM))
    xs_scr[...] = xbc[:, :SSD_INNER]
    bm_scr[...] = xbc[:, SSD_INNER:SSD_INNER + LANES]
    cm_scr[...] = xbc[:, SSD_INNER + LANES:]

    qkv = _silu(conv_cols(LRU_WIDTH + SSD_CONV_DIM, CONV_W))
    ones_bd = onesbd_ref[...]
    nq = DN_HEADS * DN_DK
    qh = qkv[:, :nq]
    kh = qkv[:, nq:2 * nq]
    q_scr[...] = qh * lax.rsqrt(_segsum_lanes(qh * qh, ones_bd) + RMS_EPS) * (DN_DK ** -0.5)
    k_scr[...] = kh * lax.rsqrt(_segsum_lanes(kh * kh, ones_bd) + RMS_EPS)
    v_scr[...] = qkv[:, 2 * nq:]

    xe_scr[0:SUBLANES, :] = xe_scr[rows:rows + SUBLANES, :]

    sl = proj_scr[:, COL_SMALL:COL_SMALL + LANES]
    sv = slabv_ref[...]
    sp = _softplus(sl + sv[0:1])
    sg = jax.nn.sigmoid(sl)
    coef = jnp.where(sv[2:3] > 0.0, -jnp.exp(sv[1:2]), 0.0)
    cs = _chunk_cumsum(sp * coef)
    dte_scr[...] = _expand_cols(sp, SM_DT, SSD_HEADS)
    cse_scr[...] = _expand_cols(cs, SM_DT, SSD_HEADS)
    be_scr[...] = _expand_cols(sg, SM_BETA, DN_HEADS)
    cg_scr[...] = _expand_cols(cs, SM_ALPHA, DN_HEADS)

    def chunk_body(c, carry):
        r0 = pl.multiple_of(c * CHUNK, CHUNK)
        rs = pl.ds(r0, CHUNK)

        a_c = la_scr[rs, :]
        b_c = lb_scr[rs, :]
        pos = _iota2((CHUNK, LRU_WIDTH), 0)
        s = 1
        while s < CHUNK:
            keep = pos >= s
            a_sh = jnp.where(keep, pltpu.roll(a_c, s, 0), 1.0)
            b_sh = jnp.where(keep, pltpu.roll(b_c, s, 0), 0.0)
            b_c = a_c * b_sh + b_c
            a_c = a_c * a_sh
            s *= 2
        hseq = a_c * lruh_scr[0:1, :] + b_c
        hl_scr[rs, :] = hseq
        lruh_scr[0:1, :] = hseq[CHUNK - 1:CHUNK, :]

        r4 = _iota2((CHUNK, 256), 0)
        c4 = _iota2((CHUNK, 256), 1) & (CHUNK - 1)
        eye4 = c4 == r4
        tril4 = c4 <= r4
        strict4 = c4 < r4
        bd4 = ((_iota2((256, 256), 0) >> 6) == (_iota2((256, 256), 1) >> 6)).astype(F32)

        def bdiag4(x):
            return _tile_rows(x, 4) * bd4

        xs = xs_scr[rs, :]
        bm = bm_scr[rs, :]
        cm = cm_scr[rs, :]
        cse = cse_scr[rs, :]
        xdt = xs * dte_scr[rs, :]
        r8 = _iota2((CHUNK, SSD_INNER), 0)
        c8 = _iota2((CHUNK, SSD_INNER), 1) & (CHUNK - 1)
        row8 = jnp.sum(jnp.where(c8 == r8, cse, 0.0), axis=0, keepdims=True)
        lmat = jnp.where(c8 <= r8, jnp.exp(jnp.minimum(cse - row8, 0.0)), 0.0)
        bd2 = ((_iota2((LANES, LANES), 0) >> 6) == (_iota2((LANES, LANES), 1) >> 6)).astype(F32)
        g2 = _dot_nt(cm, _tile_rows(bm, 2) * bd2)
        g2r = pltpu.roll(g2, 64, 1)
        lane = _iota2((CHUNK, LANES), 1)
        gg0 = jnp.where(lane < 64, g2, g2r)
        gg1 = jnp.where(lane < 64, g2r, g2)
        mx = jnp.concatenate([gg0, gg0, gg1, gg1], axis=1) * lmat
        yd = jnp.concatenate([_dot(mx[:, :256], bdiag4(xdt[:, :256])),
                              _dot(mx[:, 256:], bdiag4(xdt[:, 256:]))], axis=1)
        s_ssd = ssds_scr[...]
        yo = _dot(cm, s_ssd) * jnp.exp(cse)
        ys_scr[rs, :] = yd + yo
        last8 = cse[CHUNK - 1:CHUNK, :]
        xdec = xdt * jnp.exp(last8 - cse)
        gmask = ((_iota2((LANES, SSD_INNER), 0) >> 6) == (_iota2((LANES, SSD_INNER), 1) >> 8)).astype(F32)
        ssds_scr[...] = s_ssd * jnp.exp(last8) + _dot_tn(bm, xdec) * gmask

        q = q_scr[rs, :]
        k = k_scr[rs, :]
        be = be_scr[rs, :]
        cg = cg_scr[rs, :]
        eg = jnp.exp(cg)
        kb = k * be
        vb = v_scr[rs, :] * be
        rowg = jnp.sum(jnp.where(eye4, cg, 0.0), axis=0, keepdims=True)
        dm = jnp.where(tril4, jnp.exp(jnp.minimum(cg - rowg, 0.0)), 0.0)
        a_all = _dot_nt(jnp.concatenate([q, kb], axis=0), bdiag4(k))
        attn = a_all[:CHUNK] * dm
        m = jnp.where(strict4, a_all[CHUNK:] * dm, 0.0)
        p = -m
        t = jnp.where(eye4, 1.0, 0.0) + p
        for _ in range(5):
            p = _dot(p, bdiag4(p))
            t = t + _dot(t, bdiag4(p))
        uw = _dot(t, jnp.concatenate([bdiag4(vb), bdiag4(kb * eg)], axis=1))
        s_dn = dns_scr[...]
        ws = _dot(jnp.concatenate([uw[:, 256:], q * eg], axis=0), s_dn)
        v_new = uw[:, :256] - ws[:CHUNK]
        od_scr[rs, :] = ws[CHUNK:] + _dot(attn, bdiag4(v_new))
        lastg = cg[CHUNK - 1:CHUNK, :]
        k_dec = k * jnp.exp(lastg - cg)
        dns_scr[...] = s_dn * jnp.exp(lastg) + _dot_tn(k_dec, v_new) * bd4
        return carry

    lax.fori_loop(0, rows // CHUNK, chunk_body, 0)

    o_lru = hl_scr[...] * _gelu_tanh(proj_scr[:, COL_LRU_Y:COL_LRU_Y + LRU_WIDTH])
    ssdv = ssdv_ref[...]
    y = ys_scr[...] + ssdv[0:1] * xs_scr[...]
    y = y * _silu(proj_scr[:, COL_SSD_Z:COL_SSD_Z + SSD_INNER])
    gw = SSD_HPG * SSD_HEADDIM
    o_ssd = []
    for g in range(SSD_GROUPS):
        yg = y[:, g * gw:(g + 1) * gw]
        ms = jnp.mean(yg * yg, axis=-1, keepdims=True)
        o_ssd.append(yg * lax.rsqrt(ms + RMS_EPS) * ssdv[1:2, g * gw:(g + 1) * gw])
    od = od_scr[...]
    ms = _segsum_lanes(od * od, ones_bd) * (1.0 / DN_DV)
    o_dn = od * lax.rsqrt(ms + RMS_EPS) * dnv_ref[0:1] * _silu(proj_scr[:, COL_DN_GATE:COL_DN_GATE + DN_HEADS * DN_DV])
    mix = jnp.concatenate([o_lru] + o_ssd + [o_dn], axis=1)
    y2 = alpha * h + _dot(mix, wout_ref[...])
    o_ref[0] = _layer_norm(y2, lng_ref[...], lnb_ref[...])


def _mixer(h3d, p, *, alpha, rows):
    b, lp, d = h3d.shape
    f32_rows = (PROJ_W + CONV_W + 8 * 256 + 4 * SSD_INNER + 2 * LANES) * 4
    vmem = (rows + SUBLANES) * f32_rows + 2 * 2 * rows * d * 4 + (d * PROJ_W + d * d + 256 * 512 + 256 * 256) * 2
    vmem = min(2 * vmem + (8 << 20), V7X_VMEM_BYTES - (4 << 20))
    rf = lambda *s: pltpu.VMEM(s, F32)
    return pl.pallas_call(
        functools.partial(_mixer_kernel, alpha=alpha, rows=rows),
        grid=(b, lp // rows),
        in_specs=[
            pl.BlockSpec((1, rows, d), lambda i, t: (i, t, 0)),
            _resident((d, PROJ_W)), _resident((SUBLANES, CONV_W)), _resident((LRU_WIDTH, 2 * LRU_WIDTH)),
            _resident((1, 2 * LRU_WIDTH)), _resident((1, LRU_WIDTH)), _resident((SUBLANES, LANES)),
            _resident((SUBLANES, SSD_INNER)), _resident((SUBLANES, 256)), _resident((256, 256)),
            _resident((d, d)), _resident((1, d)), _resident((1, d)),
        ],
        out_specs=pl.BlockSpec((1, rows, d), lambda i, t: (i, t, 0)),
        out_shape=jax.ShapeDtypeStruct((b, lp, d), F32),
        scratch_shapes=[
            rf(rows, PROJ_W), rf(rows + SUBLANES, CONV_W),
            rf(rows, 256), rf(rows, 256),
            rf(rows, 256), rf(rows, 256), rf(rows, 256), rf(rows, 256), rf(rows, 256),
            rf(rows, SSD_INNER), rf(rows, SSD_INNER), rf(rows, SSD_INNER),
            rf(rows, LANES), rf(rows, LANES),
            rf(rows, 256), rf(rows, SSD_INNER), rf(rows, 256),
            rf(SUBLANES, 256), rf(LANES, SSD_INNER), rf(256, 256),
        ],
        compiler_params=pltpu.CompilerParams(dimension_semantics=("arbitrary", "arbitrary"),
                                             vmem_limit_bytes=vmem),
        name="mixer",
    )(h3d, p["win"], p["cwb"], p["wgate"], p["gateb"], p["lam"], p["slabv"], p["ssdv"], p["dnv"],
      p["ones_bd"], p["wout"], p["lng"], p["lnb"])


def _mixer_params(l, w_in, lru_conv_w, lru_conv_b, lru_w_a, lru_b_a, lru_w_x, lru_b_x, lru_lambda,
                  ssd_conv_w, ssd_conv_b, ssd_dt_bias, ssd_a_log, ssd_d, ssd_norm_w,
                  dn_conv_w, dn_a_log, dn_dt_bias, dn_norm_w, w_out, ln_g, ln_b):
    d = w_in.shape[1]
    o = 0
    parts = {}
    for name, size in (("lru_u", LRU_WIDTH), ("lru_y", LRU_WIDTH), ("ssd_z", SSD_INNER), ("ssd_xbc", SSD_CONV_DIM),
                       ("ssd_dt", SSD_HEADS), ("dn_qkv", DN_CONV_DIM), ("dn_gate", DN_HEADS * DN_DV),
                       ("dn_beta", DN_HEADS), ("dn_alpha", DN_HEADS)):
        parts[name] = w_in[l, :, o:o + size]
        o += size
    pad = jnp.zeros((d, PROJ_W - COL_SMALL - SSD_HEADS - 2 * DN_HEADS), F32)
    win = jnp.concatenate([parts["lru_u"], parts["ssd_xbc"], parts["dn_qkv"], parts["lru_y"], parts["ssd_z"],
                           parts["dn_gate"], parts["ssd_dt"], parts["dn_beta"], parts["dn_alpha"], pad],
                          axis=1).astype(BF16)
    conv_w = jnp.concatenate([lru_conv_w[l], ssd_conv_w[l], dn_conv_w[l]], axis=1)
    conv_b = jnp.concatenate([lru_conv_b[l], ssd_conv_b[l], jnp.zeros((DN_CONV_DIM,), F32)])
    cwb = jnp.concatenate([conv_w, conv_b[None], jnp.zeros((SUBLANES - CONV_K - 1, CONV_W), F32)], axis=0)
    wgate = jnp.zeros((LRU_WIDTH, 2 * LRU_WIDTH), F32)
    for hh in range(LRU_HEADS):
        sl = slice(hh * LRU_BLOCK, (hh + 1) * LRU_BLOCK)
        wgate = wgate.at[sl, sl].set(lru_w_a[l, hh])
        wgate = wgate.at[sl, LRU_WIDTH + hh * LRU_BLOCK:LRU_WIDTH + (hh + 1) * LRU_BLOCK].set(lru_w_x[l, hh])
    slab = jnp.zeros((SUBLANES, LANES), F32)
    slab = slab.at[0, SM_DT:SM_DT + SSD_HEADS].set(ssd_dt_bias[l])
    slab = slab.at[0, SM_ALPHA:SM_ALPHA + DN_HEADS].set(dn_dt_bias[l])
    slab = slab.at[1, SM_DT:SM_DT + SSD_HEADS].set(ssd_a_log[l])
    slab = slab.at[1, SM_ALPHA:SM_ALPHA + DN_HEADS].set(dn_a_log[l])
    slab = slab.at[2, SM_DT:SM_DT + SSD_HEADS].set(1.0)
    slab = slab.at[2, SM_ALPHA:SM_ALPHA + DN_HEADS].set(1.0)
    ssdv = jnp.zeros((SUBLANES, SSD_INNER), F32)
    ssdv = ssdv.at[0].set(jnp.repeat(ssd_d[l], SSD_HEADDIM))
    ssdv = ssdv.at[1].set(ssd_norm_w[l])
    dnv = jnp.zeros((SUBLANES, DN_HEADS * DN_DV), F32).at[0].set(jnp.tile(dn_norm_w[l], DN_HEADS))
    seg = jnp.arange(256) // 64
    return dict(
        win=win, cwb=cwb, wgate=wgate.astype(BF16),
        gateb=jnp.concatenate([lru_b_a[l], lru_b_x[l]])[None], lam=lru_lambda[l][None],
        slabv=slab, ssdv=ssdv, dnv=dnv, ones_bd=(seg[:, None] == seg[None, :]).astype(BF16),
        wout=w_out[l].astype(BF16), lng=ln_g[l, 1][None], lnb=ln_b[l, 1][None],
    )


def _plan(batch, length):
    rows = min((320, 256, 128, 64), key=lambda r: (-(-length // r) * r, -r))
    lp = -(-length // rows) * rows
    tm = next(t for t in (512, 256, 128, 64, 32, 16, 8) if (batch * lp) % t == 0)
    return rows, lp, tm


def _forward(x, meta, ln_g, ln_b, ffn_w_gate, ffn_w_up, ffn_w_down, w_in, lru_conv_w, lru_conv_b, lru_w_a, lru_b_a,
             lru_w_x, lru_b_x, lru_lambda, ssd_conv_w, ssd_conv_b, ssd_dt_bias, ssd_a_log, ssd_d, ssd_norm_w,
             dn_conv_w, dn_a_log, dn_dt_bias, dn_norm_w, w_out, plan=None):
    bsz, seq, d = x.shape
    depth = ln_g.shape[0]
    alpha = float((2 * depth) ** 0.25)
    length = N_META + seq
    rows, lp, tm = plan or _plan(bsz, length)
    h = jnp.concatenate([jnp.broadcast_to(meta.astype(x.dtype)[None], (bsz, N_META, d)), x,
                         jnp.zeros((bsz, lp - length, d), x.dtype)], axis=1)
    wg = ffn_w_gate.astype(BF16)
    wu = ffn_w_up.astype(BF16)
    wd = ffn_w_down.astype(BF16)
    for l in range(depth):
        h = _ffn_ln(h.reshape(bsz * lp, d), wg[l, 0], wu[l, 0], wd[l, 0], ln_g[l, 0][None], ln_b[l, 0][None],
                    alpha=alpha, tm=tm).reshape(bsz, lp, d)
        mp = _mixer_params(l, w_in, lru_conv_w, lru_conv_b, lru_w_a, lru_b_a, lru_w_x, lru_b_x, lru_lambda,
                           ssd_conv_w, ssd_conv_b, ssd_dt_bias, ssd_a_log, ssd_d, ssd_norm_w,
                           dn_conv_w, dn_a_log, dn_dt_bias, dn_norm_w, w_out, ln_g, ln_b)
        h = _mixer(h, mp, alpha=alpha, rows=rows)
        h = _ffn_ln(h.reshape(bsz * lp, d), wg[l, 1], wu[l, 1], wd[l, 1], ln_g[l, 2][None], ln_b[l, 2][None],
                    alpha=alpha, tm=tm).reshape(bsz, lp, d)
    return h[:, N_META:length]


def kernel(x, meta, ln_g, ln_b, ffn_w_gate, ffn_w_up, ffn_w_down, w_in, lru_conv_w, lru_conv_b, lru_w_a, lru_b_a,
           lru_w_x, lru_b_x, lru_lambda, ssd_conv_w, ssd_conv_b, ssd_dt_bias, ssd_a_log, ssd_d, ssd_norm_w,
           dn_conv_w, dn_a_log, dn_dt_bias, dn_norm_w, w_out):
    return _forward(x, meta, ln_g, ln_b, ffn_w_gate, ffn_w_up, ffn_w_down, w_in, lru_conv_w, lru_conv_b, lru_w_a,
                    lru_b_a, lru_w_x, lru_b_x, lru_lambda, ssd_conv_w, ssd_conv_b, ssd_dt_bias, ssd_a_log, ssd_d,
                    ssd_norm_w, dn_conv_w, dn_a_log, dn_dt_bias, dn_norm_w, w_out)
```

```python
import functools

import jax
import jax.numpy as jnp
from jax import lax
from jax.experimental import pallas as pl
from jax.experimental.pallas import tpu as pltpu

F32 = jnp.float32
BF16 = jnp.bfloat16

D_MODEL = 1024
N_META = 16
CONV_K = 4
D_FF = 2816
LRU_WIDTH = 256
LRU_HEADS = 4
LRU_BLOCK = 64
LRU_C = 8.0
SSD_HEADS = 8
SSD_HEADDIM = 64
SSD_INNER = 512
SSD_GROUPS = 2
SSD_HPG = 4
SSD_STATE = 64
SSD_CONV_DIM = 768
DN_HEADS = 4
DN_DK = 64
DN_DV = 64
DN_CONV_DIM = 768
FFN_RES = 0.5
LN_EPS = 1e-5
RMS_EPS = 1e-6

CONV_W = LRU_WIDTH + SSD_CONV_DIM + DN_CONV_DIM
COL_LRU_Y = CONV_W
COL_SSD_Z = COL_LRU_Y + LRU_WIDTH
COL_DN_GATE = COL_SSD_Z + SSD_INNER
COL_SMALL = COL_DN_GATE + DN_HEADS * DN_DV
LANES = 128
SUBLANES = 8
PROJ_W = COL_SMALL + LANES
SM_DT, SM_BETA, SM_ALPHA = 0, SSD_HEADS, SSD_HEADS + DN_HEADS
CHUNK = 64
V7X_VMEM_BYTES = 64 * 1024 * 1024


def _dot(a, b):
    return jnp.dot(a.astype(BF16), b.astype(BF16), preferred_element_type=F32)


def _dot_nt(a, b):
    return lax.dot_general(a.astype(BF16), b.astype(BF16), (((1,), (1,)), ((), ())), preferred_element_type=F32)


def _dot_tn(a, b):
    return lax.dot_general(a.astype(BF16), b.astype(BF16), (((0,), (0,)), ((), ())), preferred_element_type=F32)


def _segsum_lanes(x, ones_bd):
    hi = x.astype(BF16)
    lo = (x - hi.astype(F32)).astype(BF16)
    return (jnp.dot(hi, ones_bd, preferred_element_type=F32) + jnp.dot(lo, ones_bd, preferred_element_type=F32))


def _sigmoid(x):
    return 0.5 * jnp.tanh(0.5 * x) + 0.5


def _silu(x):
    return x * _sigmoid(x)


def _softplus(x):
    return jnp.maximum(x, 0.0) + jnp.log1p(jnp.exp(-jnp.abs(x)))


def _gelu_tanh(x):
    return 0.5 * x * (1.0 + jnp.tanh(0.7978845608028654 * (x + 0.044715 * (x * x * x))))


def _layer_norm(y, g, b):
    mu = jnp.mean(y, axis=-1, keepdims=True)
    yc = y - mu
    var = jnp.mean(yc * yc, axis=-1, keepdims=True)
    return yc * lax.rsqrt(var + LN_EPS) * g + b


def _iota2(shape, dim):
    return lax.broadcasted_iota(jnp.int32, shape, dim)


def _tile_rows(x, n):
    return jnp.concatenate([x] * n, axis=0)


def _expand_cols(x, c0, n):
    rows = x.shape[0]
    lane = _iota2((rows, LANES), 1)
    outs = []
    for j in range(n // 2):
        a = jnp.broadcast_to(x[:, c0 + 2 * j:c0 + 2 * j + 1], (rows, LANES))
        b = jnp.broadcast_to(x[:, c0 + 2 * j + 1:c0 + 2 * j + 2], (rows, LANES))
        outs.append(jnp.where(lane < 64, a, b))
    return jnp.concatenate(outs, axis=1)


def _chunk_cumsum(x):
    pos = _iota2(x.shape, 0) & (CHUNK - 1)
    s = 1
    while s < SUBLANES:
        x = x + jnp.where(pos >= s, pltpu.roll(x, s, 0), 0.0)
        s *= 2
    parts = [x[r0:r0 + CHUNK] for r0 in range(0, x.shape[0], CHUNK)]
    while s < CHUNK:
        parts = [jnp.concatenate([p[:s], p[s:] + p[:-s]], axis=0) for p in parts]
        s *= 2
    return jnp.concatenate(parts, axis=0)


def _ffn_ln_kernel(h_ref, wg_ref, wu_ref, wd_ref, lng_ref, lnb_ref, o_ref, act_ref, *, alpha, ff_chunk):
    h = h_ref[...]
    xb = h.astype(BF16)
    d_ff = wg_ref.shape[1]
    for c0 in range(0, d_ff, ff_chunk):
        g = jnp.dot(xb, wg_ref[:, c0:c0 + ff_chunk], preferred_element_type=F32)
        u = jnp.dot(xb, wu_ref[:, c0:c0 + ff_chunk], preferred_element_type=F32)
        act_ref[:, c0:c0 + ff_chunk] = (_silu(g) * u).astype(BF16)
    y = jnp.dot(act_ref[...], wd_ref[...], preferred_element_type=F32)
    o_ref[...] = _layer_norm(alpha * h + FFN_RES * y, lng_ref[...], lnb_ref[...])


def _resident(shape):
    return pl.BlockSpec(shape, lambda *_: (0,) * len(shape), pipeline_mode=pl.Buffered(1))


def _ffn_ln(h2d, wg, wu, wd, lng, lnb, *, alpha, tm):
    rows, d = h2d.shape
    d_ff = wg.shape[1]
    ff_chunk = 256
    weights = 3 * d * d_ff * 2
    tiles = 2 * 2 * tm * d * 4 + tm * d_ff * 2 + 4 * tm * ff_chunk * 4 + 2 * tm * d * 4
    vmem = min(weights + tiles + (8 << 20), V7X_VMEM_BYTES - (4 << 20))
    return pl.pallas_call(
        functools.partial(_ffn_ln_kernel, alpha=alpha, ff_chunk=ff_chunk),
        grid=(rows // tm,),
        in_specs=[
            pl.BlockSpec((tm, d), lambda i: (i, 0)),
            _resident((d, d_ff)), _resident((d, d_ff)), _resident((d_ff, d)),
            _resident((1, d)), _resident((1, d)),
        ],
        out_specs=pl.BlockSpec((tm, d), lambda i: (i, 0)),
        out_shape=jax.ShapeDtypeStruct((rows, d), F32),
        scratch_shapes=[pltpu.VMEM((tm, d_ff), BF16)],
        compiler_params=pltpu.CompilerParams(dimension_semantics=("arbitrary",), vmem_limit_bytes=vmem),
        name="ffn_ln",
    )(h2d, wg, wu, wd, lng, lnb)


def _mixer_kernel(h_ref, win_ref, cwb_ref, wgate_ref, gateb_ref, lam_ref, slabv_ref, ssdv_ref, dnv_ref,
                  onesbd_ref, wout_ref, lng_ref, lnb_ref, o_ref,
                  proj_scr, xe_scr, la_scr, lb_scr, q_scr, k_scr, v_scr, be_scr, cg_scr,
                  xs_scr, dte_scr, cse_scr, bm_scr, cm_scr, hl_scr, ys_scr, od_scr,
                  lruh_scr, ssds_scr, dns_scr, *, alpha, rows):
    tb = pl.program_id(1)

    @pl.when(tb == 0)
    def _():
        xe_scr[0:SUBLANES, :] = jnp.zeros((SUBLANES, CONV_W), F32)
        lruh_scr[...] = jnp.zeros_like(lruh_scr)
        ssds_scr[...] = jnp.zeros_like(ssds_scr)
        dns_scr[...] = jnp.zeros_like(dns_scr)

    h = h_ref[0]
    hb = h.astype(BF16)

    def project_conv_cols(c0, c1):
        xe_scr[SUBLANES:SUBLANES + rows, c0:c1] = jnp.dot(hb, win_ref[:, c0:c1], preferred_element_type=F32)

    def conv_cols(c0, c1):
        w = cwb_ref[:, c0:c1]
        y = xe_scr[SUBLANES:SUBLANES + rows, c0:c1] * w[3:4] + w[4:5]
        for s in (1, 2, 3):
            y = y + xe_scr[SUBLANES - s:SUBLANES - s + rows, c0:c1] * w[3 - s:4 - s]
        return y

    project_conv_cols(0, LRU_WIDTH)
    project_conv_cols(LRU_WIDTH, LRU_WIDTH + SSD_CONV_DIM)
    project_conv_cols(LRU_WIDTH + SSD_CONV_DIM, CONV_W)
    proj_scr[:, COL_SMALL - CONV_W:] = jnp.dot(hb, win_ref[:, COL_SMALL:], preferred_element_type=F32)

    u = conv_cols(0, LRU_WIDTH)
    gates = _dot(u, wgate_ref[...]) + gateb_ref[...]
    r = _sigmoid(gates[:, :LRU_WIDTH])
    i_gate = _sigmoid(gates[:, LRU_WIDTH:])
    log_a = (-LRU_C) * r * _softplus(-lam_ref[...])
    a = jnp.exp(log_a)
    la_scr[...] = a
    lb_scr[...] = jnp.sqrt(1.0 - a * a) * (i_gate * u)

    xbc = _silu(conv_cols(LRU_WIDTH, LRU_WIDTH + SSD_CONV_DIM))
    xs_scr[...] = xbc[:, :SSD_INNER]
    bm_scr[...] = xbc[:, SSD_INNER:SSD_INNER + LANES]
    cm_scr[...] = xbc[:, SSD_INNER + LANES:]

    qkv = _silu(conv_cols(LRU_WIDTH + SSD_CONV_DIM, CONV_W))
    ones_bd = onesbd_ref[...]
    nq = DN_HEADS * DN_DK
    qh = qkv[:, :nq]
    kh = qkv[:, nq:2 * nq]
    q_scr[...] = qh * lax.rsqrt(_segsum_lanes(qh * qh, ones_bd) + RMS_EPS) * (DN_DK ** -0.5)
    k_scr[...] = kh * lax.rsqrt(_segsum_lanes(kh * kh, ones_bd) + RMS_EPS)
    v_scr[...] = qkv[:, 2 * nq:]

    proj_scr[:, :COL_SMALL - CONV_W] = jnp.dot(hb, win_ref[:, CONV_W:COL_SMALL], preferred_element_type=F32)

    xe_scr[0:SUBLANES, :] = xe_scr[rows:rows + SUBLANES, :]

    sl = proj_scr[:, COL_SMALL - CONV_W:]
    sv = slabv_ref[...]
    sp = _softplus(sl + sv[0:1])
    sg = _sigmoid(sl)
    coef = jnp.where(sv[2:3] > 0.0, -jnp.exp(sv[1:2]), 0.0)
    cs = _chunk_cumsum(sp * coef)
    dte_scr[...] = _expand_cols(sp, SM_DT, SSD_HEADS)
    cse_scr[...] = _expand_cols(cs, SM_DT, SSD_HEADS)
    be_scr[...] = _expand_cols(sg, SM_BETA, DN_HEADS)
    cg_scr[...] = _expand_cols(cs, SM_ALPHA, DN_HEADS)

    r4 = _iota2((CHUNK, 256), 0)
    c4 = _iota2((CHUNK, 256), 1) & (CHUNK - 1)
    eye4 = c4 == r4
    tril4 = c4 <= r4
    strict4 = c4 < r4
    bd4_mask = (_iota2((256, 256), 0) >> 6) == (_iota2((256, 256), 1) >> 6)
    bd4 = bd4_mask.astype(F32)
    bd4_b = bd4_mask.astype(BF16)
    r8 = _iota2((CHUNK, SSD_INNER), 0)
    c8 = _iota2((CHUNK, SSD_INNER), 1) & (CHUNK - 1)
    eye8 = c8 == r8
    tril8 = c8 <= r8
    bd2_b = ((_iota2((LANES, LANES), 0) >> 6) == (_iota2((LANES, LANES), 1) >> 6)).astype(BF16)
    gmask = ((_iota2((LANES, SSD_INNER), 0) >> 6) == (_iota2((LANES, SSD_INNER), 1) >> 8)).astype(F32)
    lane_lo = _iota2((CHUNK, LANES), 1) < 64
    pos = _iota2((CHUNK, LRU_WIDTH), 0)

    def bdiag4(x):
        return _tile_rows(x.astype(BF16), 4) * bd4_b

    nch = rows // CHUNK
    rsl = [slice(c * CHUNK, (c + 1) * CHUNK) for c in range(nch)]
    eye4f = jnp.where(eye4, 1.0, 0.0)


    dn = []
    for rs in rsl:
        q = q_scr[rs, :]
        k = k_scr[rs, :]
        be = be_scr[rs, :]
        cg = cg_scr[rs, :]
        eg = jnp.exp(cg)
        kb = k * be
        vb = v_scr[rs, :] * be
        rowg = jnp.sum(jnp.where(eye4, cg, 0.0), axis=0, keepdims=True)
        dm = jnp.where(tril4, jnp.exp(jnp.minimum(cg - rowg, 0.0)), 0.0)
        a_all = _dot_nt(jnp.concatenate([q, kb], axis=0), bdiag4(k))
        lastg = cg[CHUNK - 1:CHUNK, :]
        dn.append(dict(attn=a_all[:CHUNK] * dm, m=jnp.where(strict4, a_all[CHUNK:] * dm, 0.0),
                       rhs=jnp.concatenate([bdiag4(vb), bdiag4(kb * eg)], axis=1), qe=q * eg,
                       k_dec=k * jnp.exp(lastg - cg), dec=jnp.exp(lastg)))
    ps = [-d["m"] for d in dn]
    ts = [eye4f + p for p in ps]
    bps = [bdiag4(p) for p in ps]
    for _ in range(5):
        ps = [_dot(p, bp) for p, bp in zip(ps, bps)]
        bps = [bdiag4(p) for p in ps]
        ts = [t + _dot(t, bp) for t, bp in zip(ts, bps)]
    uws = [_dot(t, d["rhs"]) for t, d in zip(ts, dn)]

    ssd = []
    for rs in rsl:
        bm = bm_scr[rs, :]
        cm = cm_scr[rs, :]
        cse = cse_scr[rs, :]
        xdt = xs_scr[rs, :] * dte_scr[rs, :]
        row8 = jnp.sum(jnp.where(eye8, cse, 0.0), axis=0, keepdims=True)
        lmat = jnp.where(tril8, jnp.exp(jnp.minimum(cse - row8, 0.0)), 0.0)
        g2 = _dot_nt(cm, _tile_rows(bm.astype(BF16), 2) * bd2_b)
        g2r = pltpu.roll(g2, 64, 1)
        gg0 = jnp.where(lane_lo, g2, g2r)
        gg1 = jnp.where(lane_lo, g2r, g2)
        mx = jnp.concatenate([gg0, gg0, gg1, gg1], axis=1) * lmat
        yd = jnp.concatenate([_dot(mx[:, :256], bdiag4(xdt[:, :256])),
                              _dot(mx[:, 256:], bdiag4(xdt[:, 256:]))], axis=1)
        last8 = cse[CHUNK - 1:CHUNK, :]
        ssd.append(dict(yd=yd, cm=cm, ecs=jnp.exp(cse), dec=jnp.exp(last8),
                        upd=_dot_tn(bm, xdt * jnp.exp(last8 - cse)) * gmask))

    lru = []
    for rs in rsl:
        a_c = la_scr[rs, :]
        b_c = lb_scr[rs, :]
        s = 1
        while s < SUBLANES:
            keep = pos >= s
            a_sh = jnp.where(keep, pltpu.roll(a_c, s, 0), 1.0)
            b_sh = jnp.where(keep, pltpu.roll(b_c, s, 0), 0.0)
            b_c = a_c * b_sh + b_c
            a_c = a_c * a_sh
            s *= 2
        while s < CHUNK:
            b_c = jnp.concatenate([b_c[:s], a_c[s:] * b_c[:-s] + b_c[s:]], axis=0)
            a_c = jnp.concatenate([a_c[:s], a_c[s:] * a_c[:-s]], axis=0)
            s *= 2
        lru.append((a_c, b_c))

    lru_h = lruh_scr[0:1, :]
    s_ssd = ssds_scr[...]
    s_dn = dns_scr[...]
    for c, rs in enumerate(rsl):
        hseq = lru[c][0] * lru_h + lru[c][1]
        hl_scr[rs, :] = hseq
        lru_h = hseq[CHUNK - 1:CHUNK, :]

        sd = ssd[c]
        ys_scr[rs, :] = sd["yd"] + _dot(sd["cm"], s_ssd) * sd["ecs"]
        s_ssd = s_ssd * sd["dec"] + sd["upd"]

        d = dn[c]
        ws = _dot(jnp.concatenate([uws[c][:, 256:], d["qe"]], axis=0), s_dn)
        v_new = uws[c][:, :256] - ws[:CHUNK]
        od_scr[rs, :] = ws[CHUNK:] + _dot(d["attn"], bdiag4(v_new))
        s_dn = s_dn * d["dec"] + _dot_tn(d["k_dec"], v_new) * bd4

    lruh_scr[0:1, :] = lru_h
    ssds_scr[...] = s_ssd
    dns_scr[...] = s_dn

    o_lru = hl_scr[...] * _gelu_tanh(proj_scr[:, COL_LRU_Y - CONV_W:COL_SSD_Z - CONV_W])
    ssdv = ssdv_ref[...]
    y = ys_scr[...] + ssdv[0:1] * xs_scr[...]
    y = y * _silu(proj_scr[:, COL_SSD_Z - CONV_W:COL_DN_GATE - CONV_W])
    gw = SSD_HPG * SSD_HEADDIM
    o_ssd = []
    for g in range(SSD_GROUPS):
        yg = y[:, g * gw:(g + 1) * gw]
        ms = jnp.mean(yg * yg, axis=-1, keepdims=True)
        o_ssd.append(yg * lax.rsqrt(ms + RMS_EPS) * ssdv[1:2, g * gw:(g + 1) * gw])
    od = od_scr[...]
    ms = _segsum_lanes(od * od, ones_bd) * (1.0 / DN_DV)
    o_dn = od * lax.rsqrt(ms + RMS_EPS) * dnv_ref[0:1] * _silu(proj_scr[:, COL_DN_GATE - CONV_W:COL_SMALL - CONV_W])
    mix = jnp.concatenate([o_lru] + o_ssd + [o_dn], axis=1)
    y2 = alpha * h + _dot(mix, wout_ref[...])
    o_ref[0] = _layer_norm(y2, lng_ref[...], lnb_ref[...])


def _mixer(h3d, p, *, alpha, rows):
    b, lp, d = h3d.shape
    f32_rows = (PROJ_W + CONV_W + 8 * 256 + 4 * SSD_INNER + 2 * LANES) * 4
    vmem = (rows + SUBLANES) * f32_rows + 2 * 2 * rows * d * 4 + (d * PROJ_W + d * d + 256 * 512 + 256 * 256) * 2
    vmem = min(2 * vmem + (8 << 20), V7X_VMEM_BYTES - (4 << 20))
    rf = lambda *s: pltpu.VMEM(s, F32)
    return pl.pallas_call(
        functools.partial(_mixer_kernel, alpha=alpha, rows=rows),
        grid=(b, lp // rows),
        in_specs=[
            pl.BlockSpec((1, rows, d), lambda i, t: (i, t, 0)),
            _resident((d, PROJ_W)), _resident((SUBLANES, CONV_W)), _resident((LRU_WIDTH, 2 * LRU_WIDTH)),
            _resident((1, 2 * LRU_WIDTH)), _resident((1, LRU_WIDTH)), _resident((SUBLANES, LANES)),
            _resident((SUBLANES, SSD_INNER)), _resident((SUBLANES, 256)), _resident((256, 256)),
            _resident((d, d)), _resident((1, d)), _resident((1, d)),
        ],
        out_specs=pl.BlockSpec((1, rows, d), lambda i, t: (i, t, 0)),
        out_shape=jax.ShapeDtypeStruct((b, lp, d), F32),
        scratch_shapes=[
            rf(rows, PROJ_W - CONV_W), rf(rows + SUBLANES, CONV_W),
            rf(rows, 256), rf(rows, 256),
            rf(rows, 256), rf(rows, 256), rf(rows, 256), rf(rows, 256), rf(rows, 256),
            rf(rows, SSD_INNER), rf(rows, SSD_INNER), rf(rows, SSD_INNER),
            rf(rows, LANES), rf(rows, LANES),
            rf(rows, 256), rf(rows, SSD_INNER), rf(rows, 256),
            rf(SUBLANES, 256), rf(LANES, SSD_INNER), rf(256, 256),
        ],
        compiler_params=pltpu.CompilerParams(dimension_semantics=("arbitrary", "arbitrary"),
                                             vmem_limit_bytes=vmem),
        name="mixer",
    )(h3d, p["win"], p["cwb"], p["wgate"], p["gateb"], p["lam"], p["slabv"], p["ssdv"], p["dnv"],
      p["ones_bd"], p["wout"], p["lng"], p["lnb"])


def _mixer_params(l, w_in, lru_conv_w, lru_conv_b, lru_w_a, lru_b_a, lru_w_x, lru_b_x, lru_lambda,
                  ssd_conv_w, ssd_conv_b, ssd_dt_bias, ssd_a_log, ssd_d, ssd_norm_w,
                  dn_conv_w, dn_a_log, dn_dt_bias, dn_norm_w, w_out, ln_g, ln_b):
    d = w_in.shape[1]
    o = 0
    parts = {}
    for name, size in (("lru_u", LRU_WIDTH), ("lru_y", LRU_WIDTH), ("ssd_z", SSD_INNER), ("ssd_xbc", SSD_CONV_DIM),
                       ("ssd_dt", SSD_HEADS), ("dn_qkv", DN_CONV_DIM), ("dn_gate", DN_HEADS * DN_DV),
                       ("dn_beta", DN_HEADS), ("dn_alpha", DN_HEADS)):
        parts[name] = w_in[l, :, o:o + size]
        o += size
    pad = jnp.zeros((d, PROJ_W - COL_SMALL - SSD_HEADS - 2 * DN_HEADS), F32)
    win = jnp.concatenate([parts["lru_u"], parts["ssd_xbc"], parts["dn_qkv"], parts["lru_y"], parts["ssd_z"],
                           parts["dn_gate"], parts["ssd_dt"], parts["dn_beta"], parts["dn_alpha"], pad],
                          axis=1).astype(BF16)
    conv_w = jnp.concatenate([lru_conv_w[l], ssd_conv_w[l], dn_conv_w[l]], axis=1)
    conv_b = jnp.concatenate([lru_conv_b[l], ssd_conv_b[l], jnp.zeros((DN_CONV_DIM,), F32)])
    cwb = jnp.concatenate([conv_w, conv_b[None], jnp.zeros((SUBLANES - CONV_K - 1, CONV_W), F32)], axis=0)
    wgate = jnp.zeros((LRU_WIDTH, 2 * LRU_WIDTH), F32)
    for hh in range(LRU_HEADS):
        sl = slice(hh * LRU_BLOCK, (hh + 1) * LRU_BLOCK)
        wgate = wgate.at[sl, sl].set(lru_w_a[l, hh])
        wgate = wgate.at[sl, LRU_WIDTH + hh * LRU_BLOCK:LRU_WIDTH + (hh + 1) * LRU_BLOCK].set(lru_w_x[l, hh])
    slab = jnp.zeros((SUBLANES, LANES), F32)
    slab = slab.at[0, SM_DT:SM_DT + SSD_HEADS].set(ssd_dt_bias[l])
    slab = slab.at[0, SM_ALPHA:SM_ALPHA + DN_HEADS].set(dn_dt_bias[l])
    slab = slab.at[1, SM_DT:SM_DT + SSD_HEADS].set(ssd_a_log[l])
    slab = slab.at[1, SM_ALPHA:SM_ALPHA + DN_HEADS].set(dn_a_log[l])
    slab = slab.at[2, SM_DT:SM_DT + SSD_HEADS].set(1.0)
    slab = slab.at[2, SM_ALPHA:SM_ALPHA + DN_HEADS].set(1.0)
    ssdv = jnp.zeros((SUBLANES, SSD_INNER), F32)
    ssdv = ssdv.at[0].set(jnp.repeat(ssd_d[l], SSD_HEADDIM))
    ssdv = ssdv.at[1].set(ssd_norm_w[l])
    dnv = jnp.zeros((SUBLANES, DN_HEADS * DN_DV), F32).at[0].set(jnp.tile(dn_norm_w[l], DN_HEADS))
    seg = jnp.arange(256) // 64
    return dict(
        win=win, cwb=cwb, wgate=wgate.astype(BF16),
        gateb=jnp.concatenate([lru_b_a[l], lru_b_x[l]])[None], lam=lru_lambda[l][None],
        slabv=slab, ssdv=ssdv, dnv=dnv, ones_bd=(seg[:, None] == seg[None, :]).astype(BF16),
        wout=w_out[l].astype(BF16), lng=ln_g[l, 1][None], lnb=ln_b[l, 1][None],
    )


def _plan(batch, length):
    rows = min((320, 256, 128, 64), key=lambda r: (-(-length // r) * r, -r))
    lp = -(-length // rows) * rows
    tm = next(t for t in (512, 256, 128, 64, 32, 16, 8) if (batch * lp) % t == 0)
    return rows, lp, tm


def _forward(x, meta, ln_g, ln_b, ffn_w_gate, ffn_w_up, ffn_w_down, w_in, lru_conv_w, lru_conv_b, lru_w_a, lru_b_a,
             lru_w_x, lru_b_x, lru_lambda, ssd_conv_w, ssd_conv_b, ssd_dt_bias, ssd_a_log, ssd_d, ssd_norm_w,
             dn_conv_w, dn_a_log, dn_dt_bias, dn_norm_w, w_out, plan=None):
    bsz, seq, d = x.shape
    depth = ln_g.shape[0]
    alpha = float((2 * depth) ** 0.25)
    length = N_META + seq
    rows, lp, tm = plan or _plan(bsz, length)
    h = jnp.concatenate([jnp.broadcast_to(meta.astype(x.dtype)[None], (bsz, N_META, d)), x,
                         jnp.zeros((bsz, lp - length, d), x.dtype)], axis=1)
    wg = ffn_w_gate.astype(BF16)
    wu = ffn_w_up.astype(BF16)
    wd = ffn_w_down.astype(BF16)
    for l in range(depth):
        h = _ffn_ln(h.reshape(bsz * lp, d), wg[l, 0], wu[l, 0], wd[l, 0], ln_g[l, 0][None], ln_b[l, 0][None],
                    alpha=alpha, tm=tm).reshape(bsz, lp, d)
        mp = _mixer_params(l, w_in, lru_conv_w, lru_conv_b, lru_w_a, lru_b_a, lru_w_x, lru_b_x, lru_lambda,
                           ssd_conv_w, ssd_conv_b, ssd_dt_bias, ssd_a_log, ssd_d, ssd_norm_w,
                           dn_conv_w, dn_a_log, dn_dt_bias, dn_norm_w, w_out, ln_g, ln_b)
        h = _mixer(h, mp, alpha=alpha, rows=rows)
        h = _ffn_ln(h.reshape(bsz * lp, d), wg[l, 1], wu[l, 1], wd[l, 1], ln_g[l, 2][None], ln_b[l, 2][None],
                    alpha=alpha, tm=tm).reshape(bsz, lp, d)
    return h[:, N_META:length]


def kernel(x, meta, ln_g, ln_b, ffn_w_gate, ffn_w_up, ffn_w_down, w_in, lru_conv_w, lru_conv_b, lru_w_a, lru_b_a,
           lru_w_x, lru_b_x, lru_lambda, ssd_conv_w, ssd_conv_b, ssd_dt_bias, ssd_a_log, ssd_d, ssd_norm_w,
           dn_conv_w, dn_a_log, dn_dt_bias, dn_norm_w, w_out):
    return _forward(x, meta, ln_g, ln_b, ffn_w_gate, ffn_w_up, ffn_w_down, w_in, lru_conv_w, lru_conv_b, lru_w_a,
                    lru_b_a, lru_w_x, lru_b_x, lru_lambda, ssd_conv_w, ssd_conv_b, ssd_dt_bias, ssd_a_log, ssd_d,
                    ssd_norm_w, dn_conv_w, dn_a_log, dn_dt_bias, dn_norm_w, w_out)
```

```python
import functools

import jax
import jax.numpy as jnp
from jax import lax
from jax.experimental import pallas as pl
from jax.experimental.pallas import tpu as pltpu

F32 = jnp.float32
BF16 = jnp.bfloat16

D_MODEL = 1024
N_META = 16
CONV_K = 4
D_FF = 2816
LRU_WIDTH = 256
LRU_HEADS = 4
LRU_BLOCK = 64
LRU_C = 8.0
SSD_HEADS = 8
SSD_HEADDIM = 64
SSD_INNER = 512
SSD_GROUPS = 2
SSD_HPG = 4
SSD_STATE = 64
SSD_CONV_DIM = 768
DN_HEADS = 4
DN_DK = 64
DN_DV = 64
DN_CONV_DIM = 768
FFN_RES = 0.5
LN_EPS = 1e-5
RMS_EPS = 1e-6

CONV_W = LRU_WIDTH + SSD_CONV_DIM + DN_CONV_DIM
COL_LRU_Y = CONV_W
COL_SSD_Z = COL_LRU_Y + LRU_WIDTH
COL_DN_GATE = COL_SSD_Z + SSD_INNER
COL_SMALL = COL_DN_GATE + DN_HEADS * DN_DV
LANES = 128
SUBLANES = 8
PROJ_W = COL_SMALL + LANES
GATE_W = PROJ_W - CONV_W
SM_DT, SM_BETA, SM_ALPHA = 0, SSD_HEADS, SSD_HEADS + DN_HEADS
CHUNK = 64
V7X_VMEM_BYTES = 64 * 1024 * 1024


def _dot(a, b):
    return jnp.dot(a.astype(BF16), b.astype(BF16), preferred_element_type=F32)


def _dot_nt(a, b):
    return lax.dot_general(a.astype(BF16), b.astype(BF16), (((1,), (1,)), ((), ())), preferred_element_type=F32)


def _dot_tn(a, b):
    return lax.dot_general(a.astype(BF16), b.astype(BF16), (((0,), (0,)), ((), ())), preferred_element_type=F32)


def _segsum_lanes(x, ones_bd):
    hi = x.astype(BF16)
    lo = (x - hi.astype(F32)).astype(BF16)
    return (jnp.dot(hi, ones_bd, preferred_element_type=F32) + jnp.dot(lo, ones_bd, preferred_element_type=F32))


def _sigmoid(x):
    return 0.5 * jnp.tanh(0.5 * x) + 0.5


def _silu(x):
    return x * _sigmoid(x)


def _softplus(x):
    return jnp.maximum(x, 0.0) + jnp.log1p(jnp.exp(-jnp.abs(x)))


def _gelu_tanh(x):
    return 0.5 * x * (1.0 + jnp.tanh(0.7978845608028654 * (x + 0.044715 * (x * x * x))))


def _layer_norm(y, g, b):
    mu = jnp.mean(y, axis=-1, keepdims=True)
    yc = y - mu
    var = jnp.mean(yc * yc, axis=-1, keepdims=True)
    return yc * lax.rsqrt(var + LN_EPS) * g + b


def _iota2(shape, dim):
    return lax.broadcasted_iota(jnp.int32, shape, dim)


def _tile_rows(x, n):
    return jnp.concatenate([x] * n, axis=0)


def _expand_cols(x, c0, n):
    rows = x.shape[0]
    lane = _iota2((rows, LANES), 1)
    outs = []
    for j in range(n // 2):
        a = jnp.broadcast_to(x[:, c0 + 2 * j:c0 + 2 * j + 1], (rows, LANES))
        b = jnp.broadcast_to(x[:, c0 + 2 * j + 1:c0 + 2 * j + 2], (rows, LANES))
        outs.append(jnp.where(lane < 64, a, b))
    return jnp.concatenate(outs, axis=1)


def _chunk_cumsum(x):
    pos = _iota2(x.shape, 0) & (CHUNK - 1)
    s = 1
    while s < SUBLANES:
        x = x + jnp.where(pos >= s, pltpu.roll(x, s, 0), 0.0)
        s *= 2
    parts = [x[r0:r0 + CHUNK] for r0 in range(0, x.shape[0], CHUNK)]
    while s < CHUNK:
        parts = [jnp.concatenate([p[:s], p[s:] + p[:-s]], axis=0) for p in parts]
        s *= 2
    return jnp.concatenate(parts, axis=0)


def _ffn_ln_kernel(h_ref, wg_ref, wu_ref, wd_ref, lng_ref, lnb_ref, o_ref, act_ref, *, alpha, ff_chunk):
    h = h_ref[...]
    xb = h.astype(BF16)
    d_ff = wg_ref.shape[1]
    for c0 in range(0, d_ff, ff_chunk):
        g = jnp.dot(xb, wg_ref[:, c0:c0 + ff_chunk], preferred_element_type=F32)
        u = jnp.dot(xb, wu_ref[:, c0:c0 + ff_chunk], preferred_element_type=F32)
        act_ref[:, c0:c0 + ff_chunk] = (_silu(g) * u).astype(BF16)
    y = jnp.dot(act_ref[...], wd_ref[...], preferred_element_type=F32)
    o_ref[...] = _layer_norm(alpha * h + FFN_RES * y, lng_ref[...], lnb_ref[...])


def _resident(shape):
    return pl.BlockSpec(shape, lambda *_: (0,) * len(shape), pipeline_mode=pl.Buffered(1))


def _ffn_ln(h2d, wg, wu, wd, lng, lnb, *, alpha, tm):
    rows, d = h2d.shape
    d_ff = wg.shape[1]
    ff_chunk = 256
    weights = 3 * d * d_ff * 2
    tiles = 2 * 2 * tm * d * 4 + tm * d_ff * 2 + 4 * tm * ff_chunk * 4 + 2 * tm * d * 4
    vmem = min(weights + tiles + (8 << 20), V7X_VMEM_BYTES - (4 << 20))
    return pl.pallas_call(
        functools.partial(_ffn_ln_kernel, alpha=alpha, ff_chunk=ff_chunk),
        grid=(rows // tm,),
        in_specs=[
            pl.BlockSpec((tm, d), lambda i: (i, 0)),
            _resident((d, d_ff)), _resident((d, d_ff)), _resident((d_ff, d)),
            _resident((1, d)), _resident((1, d)),
        ],
        out_specs=pl.BlockSpec((tm, d), lambda i: (i, 0)),
        out_shape=jax.ShapeDtypeStruct((rows, d), F32),
        scratch_shapes=[pltpu.VMEM((tm, d_ff), BF16)],
        compiler_params=pltpu.CompilerParams(dimension_semantics=("arbitrary",), vmem_limit_bytes=vmem),
        name="ffn_ln",
    )(h2d, wg, wu, wd, lng, lnb)


def _mixer_kernel(h_ref, win_ref, cwb_ref, wgate_ref, gateb_ref, lam_ref, slabv_ref, ssdv_ref, dnv_ref,
                  onesbd_ref, wout_ref, lng_ref, lnb_ref, o_ref,
                  proj_scr, xe_scr, la_scr, lb_scr, q_scr, k_scr, v_scr, be_scr, cg_scr,
                  xs_scr, dte_scr, cse_scr, bm_scr, cm_scr, hl_scr, ys_scr, od_scr,
                  lruh_scr, ssds_scr, dns_scr, *, alpha, rows, streams):
    tb = pl.program_id(1)

    @pl.when(tb == 0)
    def _():
        xe_scr[:, 0:SUBLANES, :] = jnp.zeros((streams, SUBLANES, CONV_W), F32)
        lruh_scr[...] = jnp.zeros_like(lruh_scr)
        ssds_scr[...] = jnp.zeros_like(ssds_scr)
        dns_scr[...] = jnp.zeros_like(dns_scr)

    ones_bd = onesbd_ref[...]
    nq = DN_HEADS * DN_DK


    def prepare(si):
        xe = xe_scr.at[si]
        proj = proj_scr.at[si]
        hb = h_ref[si].astype(BF16)

        def project_conv_cols(c0, c1):
            xe[SUBLANES:SUBLANES + rows, c0:c1] = jnp.dot(hb, win_ref[:, c0:c1], preferred_element_type=F32)

        def conv_cols(c0, c1):
            w = cwb_ref[:, c0:c1]
            y = xe[SUBLANES:SUBLANES + rows, c0:c1] * w[3:4] + w[4:5]
            for s in (1, 2, 3):
                y = y + xe[SUBLANES - s:SUBLANES - s + rows, c0:c1] * w[3 - s:4 - s]
            return y

        project_conv_cols(0, LRU_WIDTH)
        project_conv_cols(LRU_WIDTH, LRU_WIDTH + SSD_CONV_DIM)
        project_conv_cols(LRU_WIDTH + SSD_CONV_DIM, CONV_W)
        proj[:, COL_SMALL - CONV_W:] = jnp.dot(hb, win_ref[:, COL_SMALL:], preferred_element_type=F32)

        u = conv_cols(0, LRU_WIDTH)
        gates = _dot(u, wgate_ref[...]) + gateb_ref[...]
        r = _sigmoid(gates[:, :LRU_WIDTH])
        i_gate = _sigmoid(gates[:, LRU_WIDTH:])
        log_a = (-LRU_C) * r * _softplus(-lam_ref[...])
        a = jnp.exp(log_a)
        la_scr[si] = a
        lb_scr[si] = jnp.sqrt(1.0 - a * a) * (i_gate * u)

        xbc = _silu(conv_cols(LRU_WIDTH, LRU_WIDTH + SSD_CONV_DIM))
        xs_scr[si] = xbc[:, :SSD_INNER]
        bm_scr[si] = xbc[:, SSD_INNER:SSD_INNER + LANES]
        cm_scr[si] = xbc[:, SSD_INNER + LANES:]

        qkv = _silu(conv_cols(LRU_WIDTH + SSD_CONV_DIM, CONV_W))
        qh = qkv[:, :nq]
        kh = qkv[:, nq:2 * nq]
        q_scr[si] = qh * lax.rsqrt(_segsum_lanes(qh * qh, ones_bd) + RMS_EPS) * (DN_DK ** -0.5)
        k_scr[si] = kh * lax.rsqrt(_segsum_lanes(kh * kh, ones_bd) + RMS_EPS)
        v_scr[si] = qkv[:, 2 * nq:]

        proj[:, :COL_SMALL - CONV_W] = jnp.dot(hb, win_ref[:, CONV_W:COL_SMALL], preferred_element_type=F32)

        xe[0:SUBLANES, :] = xe[rows:rows + SUBLANES, :]

        sl = proj[:, COL_SMALL - CONV_W:]
        sv = slabv_ref[...]
        sp = _softplus(sl + sv[0:1])
        sg = _sigmoid(sl)
        coef = jnp.where(sv[2:3] > 0.0, -jnp.exp(sv[1:2]), 0.0)
        cs = _chunk_cumsum(sp * coef)
        dte_scr[si] = _expand_cols(sp, SM_DT, SSD_HEADS)
        cse_scr[si] = _expand_cols(cs, SM_DT, SSD_HEADS)
        be_scr[si] = _expand_cols(sg, SM_BETA, DN_HEADS)
        cg_scr[si] = _expand_cols(cs, SM_ALPHA, DN_HEADS)

    for si in range(streams):
        prepare(si)

    r4 = _iota2((CHUNK, 256), 0)
    c4 = _iota2((CHUNK, 256), 1) & (CHUNK - 1)
    eye4 = c4 == r4
    tril4 = c4 <= r4
    strict4 = c4 < r4
    eye4f = jnp.where(eye4, 1.0, 0.0)
    bd4_mask = (_iota2((256, 256), 0) >> 6) == (_iota2((256, 256), 1) >> 6)
    bd4 = bd4_mask.astype(F32)
    bd4_b = bd4_mask.astype(BF16)
    r8 = _iota2((CHUNK, SSD_INNER), 0)
    c8 = _iota2((CHUNK, SSD_INNER), 1) & (CHUNK - 1)
    eye8 = c8 == r8
    tril8 = c8 <= r8
    bd2_b = ((_iota2((LANES, LANES), 0) >> 6) == (_iota2((LANES, LANES), 1) >> 6)).astype(BF16)
    gmask = ((_iota2((LANES, SSD_INNER), 0) >> 6) == (_iota2((LANES, SSD_INNER), 1) >> 8)).astype(F32)
    lane_lo = _iota2((CHUNK, LANES), 1) < 64
    pos = _iota2((CHUNK, LRU_WIDTH), 0)

    def bdiag4(x):
        return _tile_rows(x.astype(BF16), 4) * bd4_b

    nch = rows // CHUNK
    items = [(si, slice(c * CHUNK, (c + 1) * CHUNK)) for si in range(streams) for c in range(nch)]

    dn = []
    for si, rs in items:
        q = q_scr[si, rs, :]
        k = k_scr[si, rs, :]
        be = be_scr[si, rs, :]
        cg = cg_scr[si, rs, :]
        eg = jnp.exp(cg)
        kb = k * be
        vb = v_scr[si, rs, :] * be
        rowg = jnp.sum(jnp.where(eye4, cg, 0.0), axis=0, keepdims=True)
        dm = jnp.where(tril4, jnp.exp(jnp.minimum(cg - rowg, 0.0)), 0.0)
        a_all = _dot_nt(jnp.concatenate([q, kb], axis=0), bdiag4(k))
        lastg = cg[CHUNK - 1:CHUNK, :]
        dn.append(dict(attn=a_all[:CHUNK] * dm, m=jnp.where(strict4, a_all[CHUNK:] * dm, 0.0),
                       rhs=jnp.concatenate([bdiag4(vb), bdiag4(kb * eg)], axis=1), qe=q * eg,
                       k_dec=k * jnp.exp(lastg - cg), dec=jnp.exp(lastg)))
    ps = [-d["m"] for d in dn]
    ts = [eye4f + p for p in ps]
    bps = [bdiag4(p) for p in ps]
    for _ in range(5):
        ps = [_dot(p, bp) for p, bp in zip(ps, bps)]
        bps = [bdiag4(p) for p in ps]
        ts = [t + _dot(t, bp) for t, bp in zip(ts, bps)]
    uws = [_dot(t, d["rhs"]) for t, d in zip(ts, dn)]

    ssd = []
    for si, rs in items:
        bm = bm_scr[si, rs, :]
        cm = cm_scr[si, rs, :]
        cse = cse_scr[si, rs, :]
        xdt = xs_scr[si, rs, :] * dte_scr[si, rs, :]
        row8 = jnp.sum(jnp.where(eye8, cse, 0.0), axis=0, keepdims=True)
        lmat = jnp.where(tril8, jnp.exp(jnp.minimum(cse - row8, 0.0)), 0.0)
        g2 = _dot_nt(cm, _tile_rows(bm.astype(BF16), 2) * bd2_b)
        g2r = pltpu.roll(g2, 64, 1)
        gg0 = jnp.where(lane_lo, g2, g2r)
        gg1 = jnp.where(lane_lo, g2r, g2)
        mx = jnp.concatenate([gg0, gg0, gg1, gg1], axis=1) * lmat
        yd = jnp.concatenate([_dot(mx[:, :256], bdiag4(xdt[:, :256])),
                              _dot(mx[:, 256:], bdiag4(xdt[:, 256:]))], axis=1)
        last8 = cse[CHUNK - 1:CHUNK, :]
        ssd.append(dict(yd=yd, cm=cm, ecs=jnp.exp(cse), dec=jnp.exp(last8),
                        upd=_dot_tn(bm, xdt * jnp.exp(last8 - cse)) * gmask))

    lru = []
    for si, rs in items:
        a_c = la_scr[si, rs, :]
        b_c = lb_scr[si, rs, :]
        s = 1
        while s < SUBLANES:
            keep = pos >= s
            a_sh = jnp.where(keep, pltpu.roll(a_c, s, 0), 1.0)
            b_sh = jnp.where(keep, pltpu.roll(b_c, s, 0), 0.0)
            b_c = a_c * b_sh + b_c
            a_c = a_c * a_sh
            s *= 2
        while s < CHUNK:
            b_c = jnp.concatenate([b_c[:s], a_c[s:] * b_c[:-s] + b_c[s:]], axis=0)
            a_c = jnp.concatenate([a_c[:s], a_c[s:] * a_c[:-s]], axis=0)
            s *= 2
        lru.append((a_c, b_c))

    lru_h = [lruh_scr[si, 0:1, :] for si in range(streams)]
    s_ssd = [ssds_scr[si] for si in range(streams)]
    s_dn = [dns_scr[si] for si in range(streams)]
    for c in range(nch):
        for si in range(streams):
            i = si * nch + c
            rs = items[i][1]
            hseq = lru[i][0] * lru_h[si] + lru[i][1]
            hl_scr[si, rs, :] = hseq
            lru_h[si] = hseq[CHUNK - 1:CHUNK, :]

            sd = ssd[i]
            ys_scr[si, rs, :] = sd["yd"] + _dot(sd["cm"], s_ssd[si]) * sd["ecs"]
            s_ssd[si] = s_ssd[si] * sd["dec"] + sd["upd"]

            d = dn[i]
            ws = _dot(jnp.concatenate([uws[i][:, 256:], d["qe"]], axis=0), s_dn[si])
            v_new = uws[i][:, :256] - ws[:CHUNK]
            od_scr[si, rs, :] = ws[CHUNK:] + _dot(d["attn"], bdiag4(v_new))
            s_dn[si] = s_dn[si] * d["dec"] + _dot_tn(d["k_dec"], v_new) * bd4

    for si in range(streams):
        lruh_scr[si, 0:1, :] = lru_h[si]
        ssds_scr[si] = s_ssd[si]
        dns_scr[si] = s_dn[si]

    ssdv = ssdv_ref[...]
    gw = SSD_HPG * SSD_HEADDIM
    for si in range(streams):
        proj = proj_scr.at[si]
        o_lru = hl_scr[si] * _gelu_tanh(proj[:, COL_LRU_Y - CONV_W:COL_SSD_Z - CONV_W])
        y = ys_scr[si] + ssdv[0:1] * xs_scr[si]
        y = y * _silu(proj[:, COL_SSD_Z - CONV_W:COL_DN_GATE - CONV_W])
        o_ssd = []
        for g in range(SSD_GROUPS):
            yg = y[:, g * gw:(g + 1) * gw]
            ms = jnp.mean(yg * yg, axis=-1, keepdims=True)
            o_ssd.append(yg * lax.rsqrt(ms + RMS_EPS) * ssdv[1:2, g * gw:(g + 1) * gw])
        od = od_scr[si]
        ms = _segsum_lanes(od * od, ones_bd) * (1.0 / DN_DV)
        o_dn = od * lax.rsqrt(ms + RMS_EPS) * dnv_ref[0:1] * _silu(proj[:, COL_DN_GATE - CONV_W:COL_SMALL - CONV_W])
        mix = jnp.concatenate([o_lru] + o_ssd + [o_dn], axis=1)
        y2 = alpha * h_ref[si] + _dot(mix, wout_ref[...])
        o_ref[si] = _layer_norm(y2, lng_ref[...], lnb_ref[...])


def _mixer(h3d, p, *, alpha, rows, streams):
    b, lp, d = h3d.shape
    f32_row_words = GATE_W + CONV_W + 8 * 256 + 4 * SSD_INNER + 2 * LANES
    scratch = streams * (rows + SUBLANES) * f32_row_words * 4
    blocks = 2 * 2 * streams * rows * d * 4
    weights = (d * PROJ_W + d * d + 256 * 512 + 256 * 256) * 2
    vmem = min(2 * scratch + blocks + weights + (8 << 20), V7X_VMEM_BYTES - (4 << 20))
    rf = lambda *s: pltpu.VMEM((streams,) + s, F32)
    return pl.pallas_call(
        functools.partial(_mixer_kernel, alpha=alpha, rows=rows, streams=streams),
        grid=(b // streams, lp // rows),
        in_specs=[
            pl.BlockSpec((streams, rows, d), lambda i, t: (i, t, 0)),
            _resident((d, PROJ_W)), _resident((SUBLANES, CONV_W)), _resident((LRU_WIDTH, 2 * LRU_WIDTH)),
            _resident((1, 2 * LRU_WIDTH)), _resident((1, LRU_WIDTH)), _resident((SUBLANES, LANES)),
            _resident((SUBLANES, SSD_INNER)), _resident((SUBLANES, 256)), _resident((256, 256)),
            _resident((d, d)), _resident((1, d)), _resident((1, d)),
        ],
        out_specs=pl.BlockSpec((streams, rows, d), lambda i, t: (i, t, 0)),
        out_shape=jax.ShapeDtypeStruct((b, lp, d), F32),
        scratch_shapes=[
            rf(rows, GATE_W), rf(rows + SUBLANES, CONV_W),
            rf(rows, 256), rf(rows, 256),
            rf(rows, 256), rf(rows, 256), rf(rows, 256), rf(rows, 256), rf(rows, 256),
            rf(rows, SSD_INNER), rf(rows, SSD_INNER), rf(rows, SSD_INNER),
            rf(rows, LANES), rf(rows, LANES),
            rf(rows, 256), rf(rows, SSD_INNER), rf(rows, 256),
            rf(SUBLANES, 256), rf(LANES, SSD_INNER), rf(256, 256),
        ],
        compiler_params=pltpu.CompilerParams(dimension_semantics=("arbitrary", "arbitrary"),
                                             vmem_limit_bytes=vmem),
        name="mixer",
    )(h3d, p["win"], p["cwb"], p["wgate"], p["gateb"], p["lam"], p["slabv"], p["ssdv"], p["dnv"],
      p["ones_bd"], p["wout"], p["lng"], p["lnb"])


def _mixer_params(l, w_in, lru_conv_w, lru_conv_b, lru_w_a, lru_b_a, lru_w_x, lru_b_x, lru_lambda,
                  ssd_conv_w, ssd_conv_b, ssd_dt_bias, ssd_a_log, ssd_d, ssd_norm_w,
                  dn_conv_w, dn_a_log, dn_dt_bias, dn_norm_w, w_out, ln_g, ln_b):
    d = w_in.shape[1]
    o = 0
    parts = {}
    for name, size in (("lru_u", LRU_WIDTH), ("lru_y", LRU_WIDTH), ("ssd_z", SSD_INNER), ("ssd_xbc", SSD_CONV_DIM),
                       ("ssd_dt", SSD_HEADS), ("dn_qkv", DN_CONV_DIM), ("dn_gate", DN_HEADS * DN_DV),
                       ("dn_beta", DN_HEADS), ("dn_alpha", DN_HEADS)):
        parts[name] = w_in[l, :, o:o + size]
        o += size
    pad = jnp.zeros((d, PROJ_W - COL_SMALL - SSD_HEADS - 2 * DN_HEADS), F32)
    win = jnp.concatenate([parts["lru_u"], parts["ssd_xbc"], parts["dn_qkv"], parts["lru_y"], parts["ssd_z"],
                           parts["dn_gate"], parts["ssd_dt"], parts["dn_beta"], parts["dn_alpha"], pad],
                          axis=1).astype(BF16)
    conv_w = jnp.concatenate([lru_conv_w[l], ssd_conv_w[l], dn_conv_w[l]], axis=1)
    conv_b = jnp.concatenate([lru_conv_b[l], ssd_conv_b[l], jnp.zeros((DN_CONV_DIM,), F32)])
    cwb = jnp.concatenate([conv_w, conv_b[None], jnp.zeros((SUBLANES - CONV_K - 1, CONV_W), F32)], axis=0)
    wgate = jnp.zeros((LRU_WIDTH, 2 * LRU_WIDTH), F32)
    for hh in range(LRU_HEADS):
        sl = slice(hh * LRU_BLOCK, (hh + 1) * LRU_BLOCK)
        wgate = wgate.at[sl, sl].set(lru_w_a[l, hh])
        wgate = wgate.at[sl, LRU_WIDTH + hh * LRU_BLOCK:LRU_WIDTH + (hh + 1) * LRU_BLOCK].set(lru_w_x[l, hh])
    slab = jnp.zeros((SUBLANES, LANES), F32)
    slab = slab.at[0, SM_DT:SM_DT + SSD_HEADS].set(ssd_dt_bias[l])
    slab = slab.at[0, SM_ALPHA:SM_ALPHA + DN_HEADS].set(dn_dt_bias[l])
    slab = slab.at[1, SM_DT:SM_DT + SSD_HEADS].set(ssd_a_log[l])
    slab = slab.at[1, SM_ALPHA:SM_ALPHA + DN_HEADS].set(dn_a_log[l])
    slab = slab.at[2, SM_DT:SM_DT + SSD_HEADS].set(1.0)
    slab = slab.at[2, SM_ALPHA:SM_ALPHA + DN_HEADS].set(1.0)
    ssdv = jnp.zeros((SUBLANES, SSD_INNER), F32)
    ssdv = ssdv.at[0].set(jnp.repeat(ssd_d[l], SSD_HEADDIM))
    ssdv = ssdv.at[1].set(ssd_norm_w[l])
    dnv = jnp.zeros((SUBLANES, DN_HEADS * DN_DV), F32).at[0].set(jnp.tile(dn_norm_w[l], DN_HEADS))
    seg = jnp.arange(256) // 64
    return dict(
        win=win, cwb=cwb, wgate=wgate.astype(BF16),
        gateb=jnp.concatenate([lru_b_a[l], lru_b_x[l]])[None], lam=lru_lambda[l][None],
        slabv=slab, ssdv=ssdv, dnv=dnv, ones_bd=(seg[:, None] == seg[None, :]).astype(BF16),
        wout=w_out[l].astype(BF16), lng=ln_g[l, 1][None], lnb=ln_b[l, 1][None],
    )


def _plan(batch, length):
    rows = min((320, 256, 128, 64), key=lambda r: (-(-length // r) * r, -r))
    lp = -(-length // rows) * rows
    tm = next(t for t in (512, 256, 128, 64, 32, 16, 8) if (batch * lp) % t == 0)
    streams = 2 if batch % 2 == 0 else 1
    return rows, lp, tm, streams


def _forward(x, meta, ln_g, ln_b, ffn_w_gate, ffn_w_up, ffn_w_down, w_in, lru_conv_w, lru_conv_b, lru_w_a, lru_b_a,
             lru_w_x, lru_b_x, lru_lambda, ssd_conv_w, ssd_conv_b, ssd_dt_bias, ssd_a_log, ssd_d, ssd_norm_w,
             dn_conv_w, dn_a_log, dn_dt_bias, dn_norm_w, w_out, plan=None):
    bsz, seq, d = x.shape
    depth = ln_g.shape[0]
    alpha = float((2 * depth) ** 0.25)
    length = N_META + seq
    rows, lp, tm, streams = plan or _plan(bsz, length)
    h = jnp.concatenate([jnp.broadcast_to(meta.astype(x.dtype)[None], (bsz, N_META, d)), x,
                         jnp.zeros((bsz, lp - length, d), x.dtype)], axis=1)
    wg = ffn_w_gate.astype(BF16)
    wu = ffn_w_up.astype(BF16)
    wd = ffn_w_down.astype(BF16)
    for l in range(depth):
        h = _ffn_ln(h.reshape(bsz * lp, d), wg[l, 0], wu[l, 0], wd[l, 0], ln_g[l, 0][None], ln_b[l, 0][None],
                    alpha=alpha, tm=tm).reshape(bsz, lp, d)
        mp = _mixer_params(l, w_in, lru_conv_w, lru_conv_b, lru_w_a, lru_b_a, lru_w_x, lru_b_x, lru_lambda,
                           ssd_conv_w, ssd_conv_b, ssd_dt_bias, ssd_a_log, ssd_d, ssd_norm_w,
                           dn_conv_w, dn_a_log, dn_dt_bias, dn_norm_w, w_out, ln_g, ln_b)
        h = _mixer(h, mp, alpha=alpha, rows=rows, streams=streams)
        h = _ffn_ln(h.reshape(bsz * lp, d), wg[l, 1], wu[l, 1], wd[l, 1], ln_g[l, 2][None], ln_b[l, 2][None],
                    alpha=alpha, tm=tm).reshape(bsz, lp, d)
    return h[:, N_META:length]


def kernel(x, meta, ln_g, ln_b, ffn_w_gate, ffn_w_up, ffn_w_down, w_in, lru_conv_w, lru_conv_b, lru_w_a, lru_b_a,
           lru_w_x, lru_b_x, lru_lambda, ssd_conv_w, ssd_conv_b, ssd_dt_bias, ssd_a_log, ssd_d, ssd_norm_w,
           dn_conv_w, dn_a_log, dn_dt_bias, dn_norm_w, w_out):
    return _forward(x, meta, ln_g, ln_b, ffn_w_gate, ffn_w_up, ffn_w_down, w_in, lru_conv_w, lru_conv_b, lru_w_a,
                    lru_b_a, lru_w_x, lru_b_x, lru_lambda, ssd_conv_w, ssd_conv_b, ssd_dt_bias, ssd_a_log, ssd_d,
                    ssd_norm_w, dn_conv_w, dn_a_log, dn_dt_bias, dn_norm_w, w_out)
```

```python
import functools

import jax
import jax.numpy as jnp
from jax import lax
from jax.experimental import pallas as pl
from jax.experimental.pallas import tpu as pltpu

F32 = jnp.float32
BF16 = jnp.bfloat16

D_MODEL = 1024
N_META = 16
CONV_K = 4
D_FF = 2816
LRU_WIDTH = 256
LRU_HEADS = 4
LRU_BLOCK = 64
LRU_C = 8.0
SSD_HEADS = 8
SSD_HEADDIM = 64
SSD_INNER = 512
SSD_GROUPS = 2
SSD_HPG = 4
SSD_STATE = 64
SSD_CONV_DIM = 768
DN_HEADS = 4
DN_DK = 64
DN_DV = 64
DN_CONV_DIM = 768
FFN_RES = 0.5
LN_EPS = 1e-5
RMS_EPS = 1e-6

CONV_W = LRU_WIDTH + SSD_CONV_DIM + DN_CONV_DIM
COL_LRU_Y = CONV_W
COL_SSD_Z = COL_LRU_Y + LRU_WIDTH
COL_DN_GATE = COL_SSD_Z + SSD_INNER
COL_SMALL = COL_DN_GATE + DN_HEADS * DN_DV
LANES = 128
SUBLANES = 8
PROJ_W = COL_SMALL + LANES
GATE_W = PROJ_W - CONV_W
SM_DT, SM_BETA, SM_ALPHA = 0, SSD_HEADS, SSD_HEADS + DN_HEADS
CHUNK = 64
V7X_VMEM_BYTES = 64 * 1024 * 1024


def _dot(a, b):
    return jnp.dot(a.astype(BF16), b.astype(BF16), preferred_element_type=F32)


def _dot_nt(a, b):
    return lax.dot_general(a.astype(BF16), b.astype(BF16), (((1,), (1,)), ((), ())), preferred_element_type=F32)


def _dot_tn(a, b):
    return lax.dot_general(a.astype(BF16), b.astype(BF16), (((0,), (0,)), ((), ())), preferred_element_type=F32)


def _segsum_lanes(x, ones_bd):
    hi = x.astype(BF16)
    lo = (x - hi.astype(F32)).astype(BF16)
    return (jnp.dot(hi, ones_bd, preferred_element_type=F32) + jnp.dot(lo, ones_bd, preferred_element_type=F32))


def _sigmoid(x):
    return 0.5 * jnp.tanh(0.5 * x) + 0.5


def _silu(x):
    return x * _sigmoid(x)


def _softplus(x):
    return jnp.maximum(x, 0.0) + jnp.log1p(jnp.exp(-jnp.abs(x)))


def _gelu_tanh(x):
    return 0.5 * x * (1.0 + jnp.tanh(0.7978845608028654 * (x + 0.044715 * (x * x * x))))


def _layer_norm(y, g, b):
    mu = jnp.mean(y, axis=-1, keepdims=True)
    yc = y - mu
    var = jnp.mean(yc * yc, axis=-1, keepdims=True)
    return yc * lax.rsqrt(var + LN_EPS) * g + b


def _iota2(shape, dim):
    return lax.broadcasted_iota(jnp.int32, shape, dim)


def _tile_rows(x, n):
    return jnp.concatenate([x] * n, axis=0)


def _expand_cols(x, c0, n):
    rows = x.shape[0]
    lane = _iota2((rows, LANES), 1)
    outs = []
    for j in range(n // 2):
        a = jnp.broadcast_to(x[:, c0 + 2 * j:c0 + 2 * j + 1], (rows, LANES))
        b = jnp.broadcast_to(x[:, c0 + 2 * j + 1:c0 + 2 * j + 2], (rows, LANES))
        outs.append(jnp.where(lane < 64, a, b))
    return jnp.concatenate(outs, axis=1)


def _chunk_cumsum(x):
    pos = _iota2(x.shape, 0) & (CHUNK - 1)
    s = 1
    while s < SUBLANES:
        x = x + jnp.where(pos >= s, pltpu.roll(x, s, 0), 0.0)
        s *= 2
    parts = [x[r0:r0 + CHUNK] for r0 in range(0, x.shape[0], CHUNK)]
    while s < CHUNK:
        parts = [jnp.concatenate([p[:s], p[s:] + p[:-s]], axis=0) for p in parts]
        s *= 2
    return jnp.concatenate(parts, axis=0)


def _ffn_ln_kernel(h_ref, wg_ref, wu_ref, wd_ref, lng_ref, lnb_ref, o_ref, act_ref, *, alpha, ff_chunk):
    h = h_ref[...]
    xb = h.astype(BF16)
    d_ff = wg_ref.shape[1]
    for c0 in range(0, d_ff, ff_chunk):
        g = jnp.dot(xb, wg_ref[:, c0:c0 + ff_chunk], preferred_element_type=F32)
        u = jnp.dot(xb, wu_ref[:, c0:c0 + ff_chunk], preferred_element_type=F32)
        act_ref[:, c0:c0 + ff_chunk] = (_silu(g) * u).astype(BF16)
    y = jnp.dot(act_ref[...], wd_ref[...], preferred_element_type=F32)
    o_ref[...] = _layer_norm(alpha * h + FFN_RES * y, lng_ref[...], lnb_ref[...])


def _resident(shape):
    return pl.BlockSpec(shape, lambda *_: (0,) * len(shape), pipeline_mode=pl.Buffered(1))


def _ffn_ln(h2d, wg, wu, wd, lng, lnb, *, alpha, tm):
    rows, d = h2d.shape
    d_ff = wg.shape[1]
    ff_chunk = 256
    weights = 3 * d * d_ff * 2
    tiles = 2 * 2 * tm * d * 4 + tm * d_ff * 2 + 4 * tm * ff_chunk * 4 + 2 * tm * d * 4
    vmem = min(weights + tiles + (8 << 20), V7X_VMEM_BYTES - (4 << 20))
    return pl.pallas_call(
        functools.partial(_ffn_ln_kernel, alpha=alpha, ff_chunk=ff_chunk),
        grid=(rows // tm,),
        in_specs=[
            pl.BlockSpec((tm, d), lambda i: (i, 0)),
            _resident((d, d_ff)), _resident((d, d_ff)), _resident((d_ff, d)),
            _resident((1, d)), _resident((1, d)),
        ],
        out_specs=pl.BlockSpec((tm, d), lambda i: (i, 0)),
        out_shape=jax.ShapeDtypeStruct((rows, d), F32),
        scratch_shapes=[pltpu.VMEM((tm, d_ff), BF16)],
        compiler_params=pltpu.CompilerParams(dimension_semantics=("arbitrary",), vmem_limit_bytes=vmem),
        name="ffn_ln",
    )(h2d, wg, wu, wd, lng, lnb)


N_MIXER_IN = 13
N_MIXER_SCRATCH = 23
CONV_GROUPS = ((0, LRU_WIDTH), (LRU_WIDTH, LRU_WIDTH + SSD_CONV_DIM), (LRU_WIDTH + SSD_CONV_DIM, CONV_W))


def _mixer_kernel(*refs, alpha, rows, streams, nb, fused):
    (h_ref, win_ref, cwb_ref, wgate_ref, gateb_ref, lam_ref, slabv_ref, ssdv_ref, dnv_ref,
     onesbd_ref, wout_ref, lng_ref, lnb_ref) = refs[:N_MIXER_IN]
    n_in = N_MIXER_IN + (5 if fused else 0)
    o_ref = refs[n_in]
    (proj_scr, slab_scr, xel_scr, xes_scr, xed_scr, la_scr, lb_scr, q_scr, k_scr, v_scr, be_scr, cg_scr,
     xs_scr, dte_scr, cse_scr, bm_scr, cm_scr, hl_scr, ys_scr, od_scr,
     lruh_scr, ssds_scr, dns_scr) = refs[n_in + 1:n_in + 1 + N_MIXER_SCRATCH]

    ffn_units = []
    if fused:
        wg_ref, wu_ref, wd_ref, lng2_ref, lnb2_ref = refs[N_MIXER_IN:n_in]
        mid_scr, midb_scr, act_scr = refs[n_in + 1 + N_MIXER_SCRATCH:]
        step = pl.program_id(0)
        tb = lax.rem(jnp.minimum(step, pl.num_programs(0) - 2), nb)
        slot = lax.rem(step, 2)
        mid_w = mid_scr.at[slot]
        mid_r = mid_scr.at[1 - slot]
        midb_w = midb_scr.at[slot]
        midb_r = midb_scr.at[1 - slot]

        @pl.when(step == 0)
        def _():
            mid_scr[...] = jnp.zeros_like(mid_scr)
            midb_scr[...] = jnp.zeros_like(midb_scr)

        d_ff = wg_ref.shape[1]
        ff_chunk, out_chunk = 256, 256

        def act_unit(c0):
            def run():
                xb = midb_r[...]
                g = jnp.dot(xb, wg_ref[:, c0:c0 + ff_chunk], preferred_element_type=F32)
                u = jnp.dot(xb, wu_ref[:, c0:c0 + ff_chunk], preferred_element_type=F32)
                act_scr[:, c0:c0 + ff_chunk] = (_silu(g) * u).astype(BF16)
            return run

        def down_unit(n0):
            def run():
                y = jnp.dot(act_scr[...], wd_ref[:, n0:n0 + out_chunk], preferred_element_type=F32)
                o_ref[0, :, n0:n0 + out_chunk] = alpha * mid_r[:, n0:n0 + out_chunk] + FFN_RES * y
            return run

        def ln_unit():
            o_ref[0] = _layer_norm(o_ref[0], lng2_ref[...], lnb2_ref[...])

        ffn_units = ([act_unit(c0) for c0 in range(0, d_ff, ff_chunk)]
                     + [down_unit(n0) for n0 in range(0, wd_ref.shape[1], out_chunk)] + [ln_unit])
    else:
        tb = pl.program_id(1)

    def emit_ffn(n=1):
        for _ in range(n):
            if ffn_units:
                ffn_units.pop(0)()

    @pl.when(tb == 0)
    def _():
        for xe_g in (xel_scr, xes_scr, xed_scr):
            xe_g[:, 0:SUBLANES, :] = jnp.zeros((streams, SUBLANES, xe_g.shape[2]), F32)
        lruh_scr[...] = jnp.zeros_like(lruh_scr)
        ssds_scr[...] = jnp.zeros_like(ssds_scr)
        dns_scr[...] = jnp.zeros_like(dns_scr)

    ones_bd = onesbd_ref[...]
    nq = DN_HEADS * DN_DK


    def prepare(si):
        xes = (xel_scr.at[si], xes_scr.at[si], xed_scr.at[si])
        proj = proj_scr.at[si]
        slab = slab_scr.at[si]
        hb = h_ref[si].astype(BF16)

        def project_conv_cols(g):
            c0, c1 = CONV_GROUPS[g]
            xes[g][SUBLANES:SUBLANES + rows, :] = jnp.dot(hb, win_ref[:, c0:c1], preferred_element_type=F32)

        def conv_cols(g):
            c0, c1 = CONV_GROUPS[g]
            xe = xes[g]
            w = cwb_ref[:, c0:c1]
            y = xe[SUBLANES:SUBLANES + rows, :] * w[3:4] + w[4:5]
            for s in (1, 2, 3):
                y = y + xe[SUBLANES - s:SUBLANES - s + rows, :] * w[3 - s:4 - s]
            xe[0:SUBLANES, :] = xe[rows:rows + SUBLANES, :]
            return y

        project_conv_cols(0)
        project_conv_cols(1)
        project_conv_cols(2)
        slab[...] = jnp.dot(hb, win_ref[:, COL_SMALL:], preferred_element_type=F32)
        emit_ffn(3)

        u = conv_cols(0)
        gates = _dot(u, wgate_ref[...]) + gateb_ref[...]
        emit_ffn(1)
        r = _sigmoid(gates[:, :LRU_WIDTH])
        i_gate = _sigmoid(gates[:, LRU_WIDTH:])
        log_a = (-LRU_C) * r * _softplus(-lam_ref[...])
        a = jnp.exp(log_a)
        la_scr[si] = a
        lb_scr[si] = jnp.sqrt(1.0 - a * a) * (i_gate * u)

        xbc = _silu(conv_cols(1))
        xs_scr[si] = xbc[:, :SSD_INNER]
        bm_scr[si] = xbc[:, SSD_INNER:SSD_INNER + LANES]
        cm_scr[si] = xbc[:, SSD_INNER + LANES:]

        qkv = _silu(conv_cols(2))
        qh = qkv[:, :nq]
        kh = qkv[:, nq:2 * nq]
        q_scr[si] = qh * lax.rsqrt(_segsum_lanes(qh * qh, ones_bd) + RMS_EPS) * (DN_DK ** -0.5)
        k_scr[si] = kh * lax.rsqrt(_segsum_lanes(kh * kh, ones_bd) + RMS_EPS)
        v_scr[si] = qkv[:, 2 * nq:]
        emit_ffn(1)

        proj[...] = jnp.dot(hb, win_ref[:, CONV_W:COL_SMALL], preferred_element_type=F32)

        sl = slab[...]
        sv = slabv_ref[...]
        sp = _softplus(sl + sv[0:1])
        sg = _sigmoid(sl)
        coef = jnp.where(sv[2:3] > 0.0, -jnp.exp(sv[1:2]), 0.0)
        cs = _chunk_cumsum(sp * coef)
        dte_scr[si] = _expand_cols(sp, SM_DT, SSD_HEADS)
        cse_scr[si] = _expand_cols(cs, SM_DT, SSD_HEADS)
        be_scr[si] = _expand_cols(sg, SM_BETA, DN_HEADS)
        cg_scr[si] = _expand_cols(cs, SM_ALPHA, DN_HEADS)

    for si in range(streams):
        prepare(si)

    r4 = _iota2((CHUNK, 256), 0)
    c4 = _iota2((CHUNK, 256), 1) & (CHUNK - 1)
    eye4 = c4 == r4
    tril4 = c4 <= r4
    strict4 = c4 < r4
    eye4f = jnp.where(eye4, 1.0, 0.0)
    bd4_mask = (_iota2((256, 256), 0) >> 6) == (_iota2((256, 256), 1) >> 6)
    bd4 = bd4_mask.astype(F32)
    bd4_b = bd4_mask.astype(BF16)
    r8 = _iota2((CHUNK, SSD_INNER), 0)
    c8 = _iota2((CHUNK, SSD_INNER), 1) & (CHUNK - 1)
    eye8 = c8 == r8
    tril8 = c8 <= r8
    bd2_b = ((_iota2((LANES, LANES), 0) >> 6) == (_iota2((LANES, LANES), 1) >> 6)).astype(BF16)
    gmask = ((_iota2((LANES, SSD_INNER), 0) >> 6) == (_iota2((LANES, SSD_INNER), 1) >> 8)).astype(F32)
    lane_lo = _iota2((CHUNK, LANES), 1) < 64
    pos = _iota2((CHUNK, LRU_WIDTH), 0)

    def bdiag4(x):
        return _tile_rows(x.astype(BF16), 4) * bd4_b

    nch = rows // CHUNK
    items = [(si, slice(c * CHUNK, (c + 1) * CHUNK)) for si in range(streams) for c in range(nch)]

    dn = []
    for si, rs in items:
        q = q_scr[si, rs, :]
        k = k_scr[si, rs, :]
        be = be_scr[si, rs, :]
        cg = cg_scr[si, rs, :]
        eg = jnp.exp(cg)
        kb = k * be
        vb = v_scr[si, rs, :] * be
        rowg = jnp.sum(jnp.where(eye4, cg, 0.0), axis=0, keepdims=True)
        dm = jnp.where(tril4, jnp.exp(jnp.minimum(cg - rowg, 0.0)), 0.0)
        a_all = _dot_nt(jnp.concatenate([q, kb], axis=0), bdiag4(k))
        lastg = cg[CHUNK - 1:CHUNK, :]
        dn.append(dict(attn=a_all[:CHUNK] * dm, m=jnp.where(strict4, a_all[CHUNK:] * dm, 0.0),
                       rhs=jnp.concatenate([bdiag4(vb), bdiag4(kb * eg)], axis=1), qe=q * eg,
                       k_dec=k * jnp.exp(lastg - cg), dec=jnp.exp(lastg)))
    emit_ffn(1)
    ps = [-d["m"] for d in dn]
    bps = [bdiag4(p) for p in ps]
    ts = [eye4f + p for p in ps]
    ps = [_dot(p, bp) for p, bp in zip(ps, bps)]
    emit_ffn(1)
    for j in range(4):
        bps = [bdiag4(p) for p in ps]
        pt = [_dot(jnp.concatenate([p, t], axis=0), bp) for p, t, bp in zip(ps, ts, bps)]
        ps = [x[:CHUNK] for x in pt]
        ts = [t + x[CHUNK:] for t, x in zip(ts, pt)]
        emit_ffn(j % 2)
    ts = [t + _dot(t, bdiag4(p)) for t, p in zip(ts, ps)]
    uws = [_dot(t, d["rhs"]) for t, d in zip(ts, dn)]
    emit_ffn(1)

    ssd = []
    for si, rs in items:
        bm = bm_scr[si, rs, :]
        cm = cm_scr[si, rs, :]
        cse = cse_scr[si, rs, :]
        xdt = xs_scr[si, rs, :] * dte_scr[si, rs, :]
        row8 = jnp.sum(jnp.where(eye8, cse, 0.0), axis=0, keepdims=True)
        lmat = jnp.where(tril8, jnp.exp(jnp.minimum(cse - row8, 0.0)), 0.0)
        g2 = _dot_nt(cm, _tile_rows(bm.astype(BF16), 2) * bd2_b)
        g2r = pltpu.roll(g2, 64, 1)
        gg0 = jnp.where(lane_lo, g2, g2r)
        gg1 = jnp.where(lane_lo, g2r, g2)
        mx = jnp.concatenate([gg0, gg0, gg1, gg1], axis=1) * lmat
        yd = jnp.concatenate([_dot(mx[:, :256], bdiag4(xdt[:, :256])),
                              _dot(mx[:, 256:], bdiag4(xdt[:, 256:]))], axis=1)
        last8 = cse[CHUNK - 1:CHUNK, :]
        ssd.append(dict(yd=yd, cm=cm, ecs=jnp.exp(cse), dec=jnp.exp(last8),
                        upd=_dot_tn(bm, xdt * jnp.exp(last8 - cse)) * gmask))

    emit_ffn(1)

    lru = []
    for si, rs in items:
        a_c = la_scr[si, rs, :]
        b_c = lb_scr[si, rs, :]
        s = 1
        while s < SUBLANES:
            keep = pos >= s
            a_sh = jnp.where(keep, pltpu.roll(a_c, s, 0), 1.0)
            b_sh = jnp.where(keep, pltpu.roll(b_c, s, 0), 0.0)
            b_c = a_c * b_sh + b_c
            a_c = a_c * a_sh
            s *= 2
        while s < CHUNK:
            b_c = jnp.concatenate([b_c[:s], a_c[s:] * b_c[:-s] + b_c[s:]], axis=0)
            a_c = jnp.concatenate([a_c[:s], a_c[s:] * a_c[:-s]], axis=0)
            s *= 2
        lru.append((a_c, b_c))

    lru_h = [lruh_scr[si, 0:1, :] for si in range(streams)]
    s_ssd = [ssds_scr[si] for si in range(streams)]
    s_dn = [dns_scr[si] for si in range(streams)]
    for c in range(nch):
        for si in range(streams):
            i = si * nch + c
            rs = items[i][1]
            hseq = lru[i][0] * lru_h[si] + lru[i][1]
            hl_scr[si, rs, :] = hseq
            lru_h[si] = hseq[CHUNK - 1:CHUNK, :]

            sd = ssd[i]
            ys_scr[si, rs, :] = sd["yd"] + _dot(sd["cm"], s_ssd[si]) * sd["ecs"]
            s_ssd[si] = s_ssd[si] * sd["dec"] + sd["upd"]

            d = dn[i]
            ws = _dot(jnp.concatenate([uws[i][:, 256:], d["qe"]], axis=0), s_dn[si])
            v_new = uws[i][:, :256] - ws[:CHUNK]
            od_scr[si, rs, :] = ws[CHUNK:] + _dot(d["attn"], bdiag4(v_new))
            s_dn[si] = s_dn[si] * d["dec"] + _dot_tn(d["k_dec"], v_new) * bd4
        if c < nch - 1:
            emit_ffn(1)

    for si in range(streams):
        lruh_scr[si, 0:1, :] = lru_h[si]
        ssds_scr[si] = s_ssd[si]
        dns_scr[si] = s_dn[si]

    ssdv = ssdv_ref[...]
    gw = SSD_HPG * SSD_HEADDIM
    for si in range(streams):
        proj = proj_scr.at[si]
        o_lru = hl_scr[si] * _gelu_tanh(proj[:, COL_LRU_Y - CONV_W:COL_SSD_Z - CONV_W])
        y = ys_scr[si] + ssdv[0:1] * xs_scr[si]
        y = y * _silu(proj[:, COL_SSD_Z - CONV_W:COL_DN_GATE - CONV_W])
        o_ssd = []
        for g in range(SSD_GROUPS):
            yg = y[:, g * gw:(g + 1) * gw]
            ms = jnp.mean(yg * yg, axis=-1, keepdims=True)
            o_ssd.append(yg * lax.rsqrt(ms + RMS_EPS) * ssdv[1:2, g * gw:(g + 1) * gw])
        od = od_scr[si]
        ms = _segsum_lanes(od * od, ones_bd) * (1.0 / DN_DV)
        o_dn = od * lax.rsqrt(ms + RMS_EPS) * dnv_ref[0:1] * _silu(proj[:, COL_DN_GATE - CONV_W:COL_SMALL - CONV_W])
        mix = jnp.concatenate([o_lru] + o_ssd + [o_dn], axis=1)
        y2 = alpha * h_ref[si] + _dot(mix, wout_ref[...])
        if fused:
            m_out = _layer_norm(y2, lng_ref[...], lnb_ref[...])
            mid_w[...] = m_out
            midb_w[...] = m_out.astype(BF16)
        else:
            o_ref[si] = _layer_norm(y2, lng_ref[...], lnb_ref[...])
    emit_ffn(len(ffn_units))


def _mixer_scratch(rows, streams):
    rf = lambda *s: pltpu.VMEM((streams,) + s, F32)
    return [
        rf(rows, COL_SMALL - CONV_W), rf(rows, LANES),
        rf(rows + SUBLANES, LRU_WIDTH), rf(rows + SUBLANES, SSD_CONV_DIM), rf(rows + SUBLANES, DN_CONV_DIM),
        rf(rows, 256), rf(rows, 256),
        rf(rows, 256), rf(rows, 256), rf(rows, 256), rf(rows, 256), rf(rows, 256),
        rf(rows, SSD_INNER), rf(rows, SSD_INNER), rf(rows, SSD_INNER),
        rf(rows, LANES), rf(rows, LANES),
        rf(rows, 256), rf(rows, SSD_INNER), rf(rows, 256),
        rf(SUBLANES, 256), rf(LANES, SSD_INNER), rf(256, 256),
    ]


def _mixer_param_specs(d):
    return [
        _resident((d, PROJ_W)), _resident((SUBLANES, CONV_W)), _resident((LRU_WIDTH, 2 * LRU_WIDTH)),
        _resident((1, 2 * LRU_WIDTH)), _resident((1, LRU_WIDTH)), _resident((SUBLANES, LANES)),
        _resident((SUBLANES, SSD_INNER)), _resident((SUBLANES, 256)), _resident((256, 256)),
        _resident((d, d)), _resident((1, d)), _resident((1, d)),
    ]


def _mixer_param_args(p):
    return (p["win"], p["cwb"], p["wgate"], p["gateb"], p["lam"], p["slabv"], p["ssdv"], p["dnv"],
            p["ones_bd"], p["wout"], p["lng"], p["lnb"])


def _mixer_ffn(h3d, p, wg, wu, wd, lng2, lnb2, *, alpha, rows):
    b, lp, d = h3d.shape
    nb = lp // rows
    nblk = b * nb
    d_ff = wg.shape[1]
    f32_row_words = GATE_W + CONV_W + 8 * 256 + 4 * SSD_INNER + 2 * LANES
    scratch = (rows + SUBLANES) * f32_row_words * 4 + 2 * rows * d * 4 + rows * d_ff * 2
    blocks = 2 * 2 * rows * d * 4
    weights = (d * PROJ_W + d * d + 256 * 512 + 256 * 256 + 3 * d * d_ff) * 2
    vmem = min(scratch + blocks + weights + (12 << 20), V7X_VMEM_BYTES - (4 << 20))

    def in_block(s):
        m = jnp.minimum(s, nblk - 1)
        return (m // nb, m % nb, 0)

    def out_block(s):
        m = jnp.maximum(s - 1, 0)
        return (m // nb, m % nb, 0)

    return pl.pallas_call(
        functools.partial(_mixer_kernel, alpha=alpha, rows=rows, streams=1, nb=nb, fused=True),
        grid=(nblk + 1,),
        in_specs=[pl.BlockSpec((1, rows, d), in_block)] + _mixer_param_specs(d) + [
            _resident((d, d_ff)), _resident((d, d_ff)), _resident((d_ff, d)), _resident((1, d)), _resident((1, d))],
        out_specs=pl.BlockSpec((1, rows, d), out_block),
        out_shape=jax.ShapeDtypeStruct((b, lp, d), F32),
        scratch_shapes=_mixer_scratch(rows, 1) + [pltpu.VMEM((2, rows, d), F32), pltpu.VMEM((2, rows, d), BF16), pltpu.VMEM((rows, d_ff), BF16)],
        compiler_params=pltpu.CompilerParams(dimension_semantics=("arbitrary",), vmem_limit_bytes=vmem),
        name="mixer_ffn",
    )(h3d, *_mixer_param_args(p), wg, wu, wd, lng2, lnb2)


def _mixer_params(l, w_in, lru_conv_w, lru_conv_b, lru_w_a, lru_b_a, lru_w_x, lru_b_x, lru_lambda,
                  ssd_conv_w, ssd_conv_b, ssd_dt_bias, ssd_a_log, ssd_d, ssd_norm_w,
                  dn_conv_w, dn_a_log, dn_dt_bias, dn_norm_w, w_out, ln_g, ln_b):
    d = w_in.shape[1]
    o = 0
    parts = {}
    for name, size in (("lru_u", LRU_WIDTH), ("lru_y", LRU_WIDTH), ("ssd_z", SSD_INNER), ("ssd_xbc", SSD_CONV_DIM),
                       ("ssd_dt", SSD_HEADS), ("dn_qkv", DN_CONV_DIM), ("dn_gate", DN_HEADS * DN_DV),
                       ("dn_beta", DN_HEADS), ("dn_alpha", DN_HEADS)):
        parts[name] = w_in[l, :, o:o + size]
        o += size
    pad = jnp.zeros((d, PROJ_W - COL_SMALL - SSD_HEADS - 2 * DN_HEADS), F32)
    win = jnp.concatenate([parts["lru_u"], parts["ssd_xbc"], parts["dn_qkv"], parts["lru_y"], parts["ssd_z"],
                           parts["dn_gate"], parts["ssd_dt"], parts["dn_beta"], parts["dn_alpha"], pad],
                          axis=1).astype(BF16)
    conv_w = jnp.concatenate([lru_conv_w[l], ssd_conv_w[l], dn_conv_w[l]], axis=1)
    conv_b = jnp.concatenate([lru_conv_b[l], ssd_conv_b[l], jnp.zeros((DN_CONV_DIM,), F32)])
    cwb = jnp.concatenate([conv_w, conv_b[None], jnp.zeros((SUBLANES - CONV_K - 1, CONV_W), F32)], axis=0)
    wgate = jnp.zeros((LRU_WIDTH, 2 * LRU_WIDTH), F32)
    for hh in range(LRU_HEADS):
        sl = slice(hh * LRU_BLOCK, (hh + 1) * LRU_BLOCK)
        wgate = wgate.at[sl, sl].set(lru_w_a[l, hh])
        wgate = wgate.at[sl, LRU_WIDTH + hh * LRU_BLOCK:LRU_WIDTH + (hh + 1) * LRU_BLOCK].set(lru_w_x[l, hh])
    slab = jnp.zeros((SUBLANES, LANES), F32)
    slab = slab.at[0, SM_DT:SM_DT + SSD_HEADS].set(ssd_dt_bias[l])
    slab = slab.at[0, SM_ALPHA:SM_ALPHA + DN_HEADS].set(dn_dt_bias[l])
    slab = slab.at[1, SM_DT:SM_DT + SSD_HEADS].set(ssd_a_log[l])
    slab = slab.at[1, SM_ALPHA:SM_ALPHA + DN_HEADS].set(dn_a_log[l])
    slab = slab.at[2, SM_DT:SM_DT + SSD_HEADS].set(1.0)
    slab = slab.at[2, SM_ALPHA:SM_ALPHA + DN_HEADS].set(1.0)
    ssdv = jnp.zeros((SUBLANES, SSD_INNER), F32)
    ssdv = ssdv.at[0].set(jnp.repeat(ssd_d[l], SSD_HEADDIM))
    ssdv = ssdv.at[1].set(ssd_norm_w[l])
    dnv = jnp.zeros((SUBLANES, DN_HEADS * DN_DV), F32).at[0].set(jnp.tile(dn_norm_w[l], DN_HEADS))
    seg = jnp.arange(256) // 64
    return dict(
        win=win, cwb=cwb, wgate=wgate.astype(BF16),
        gateb=jnp.concatenate([lru_b_a[l], lru_b_x[l]])[None], lam=lru_lambda[l][None],
        slabv=slab, ssdv=ssdv, dnv=dnv, ones_bd=(seg[:, None] == seg[None, :]).astype(BF16),
        wout=w_out[l].astype(BF16), lng=ln_g[l, 1][None], lnb=ln_b[l, 1][None],
    )


def _plan(batch, length):
    rows = min((320, 256, 128, 64), key=lambda r: (-(-length // r) * r, -r))
    lp = -(-length // rows) * rows
    tm = next(t for t in (512, 256, 128, 64, 32, 16, 8) if (batch * lp) % t == 0)
    streams = 2 if batch % 2 == 0 else 1
    return rows, lp, tm, streams


def _forward(x, meta, ln_g, ln_b, ffn_w_gate, ffn_w_up, ffn_w_down, w_in, lru_conv_w, lru_conv_b, lru_w_a, lru_b_a,
             lru_w_x, lru_b_x, lru_lambda, ssd_conv_w, ssd_conv_b, ssd_dt_bias, ssd_a_log, ssd_d, ssd_norm_w,
             dn_conv_w, dn_a_log, dn_dt_bias, dn_norm_w, w_out, plan=None):
    bsz, seq, d = x.shape
    depth = ln_g.shape[0]
    alpha = float((2 * depth) ** 0.25)
    length = N_META + seq
    rows, lp, tm, streams = plan or _plan(bsz, length)
    h = jnp.concatenate([jnp.broadcast_to(meta.astype(x.dtype)[None], (bsz, N_META, d)), x,
                         jnp.zeros((bsz, lp - length, d), x.dtype)], axis=1)
    wg = ffn_w_gate.astype(BF16)
    wu = ffn_w_up.astype(BF16)
    wd = ffn_w_down.astype(BF16)
    for l in range(depth):
        h = _ffn_ln(h.reshape(bsz * lp, d), wg[l, 0], wu[l, 0], wd[l, 0], ln_g[l, 0][None], ln_b[l, 0][None],
                    alpha=alpha, tm=tm).reshape(bsz, lp, d)
        mp = _mixer_params(l, w_in, lru_conv_w, lru_conv_b, lru_w_a, lru_b_a, lru_w_x, lru_b_x, lru_lambda,
                           ssd_conv_w, ssd_conv_b, ssd_dt_bias, ssd_a_log, ssd_d, ssd_norm_w,
                           dn_conv_w, dn_a_log, dn_dt_bias, dn_norm_w, w_out, ln_g, ln_b)
        h = _mixer_ffn(h, mp, wg[l, 1], wu[l, 1], wd[l, 1], ln_g[l, 2][None], ln_b[l, 2][None],
                       alpha=alpha, rows=rows)
    return h[:, N_META:length]


def kernel(x, meta, ln_g, ln_b, ffn_w_gate, ffn_w_up, ffn_w_down, w_in, lru_conv_w, lru_conv_b, lru_w_a, lru_b_a,
           lru_w_x, lru_b_x, lru_lambda, ssd_conv_w, ssd_conv_b, ssd_dt_bias, ssd_a_log, ssd_d, ssd_norm_w,
           dn_conv_w, dn_a_log, dn_dt_bias, dn_norm_w, w_out):
    return _forward(x, meta, ln_g, ln_b, ffn_w_gate, ffn_w_up, ffn_w_down, w_in, lru_conv_w, lru_conv_b, lru_w_a,
                    lru_b_a, lru_w_x, lru_b_x, lru_lambda, ssd_conv_w, ssd_conv_b, ssd_dt_bias, ssd_a_log, ssd_d,
                    ssd_norm_w, dn_conv_w, dn_a_log, dn_dt_bias, dn_norm_w, w_out)
```

```python
import functools

import jax
import jax.numpy as jnp
from jax import lax
from jax.experimental import pallas as pl
from jax.experimental.pallas import tpu as pltpu

F32 = jnp.float32
BF16 = jnp.bfloat16

D_MODEL = 1024
N_META = 16
CONV_K = 4
D_FF = 2816
LRU_WIDTH = 256
LRU_HEADS = 4
LRU_BLOCK = 64
LRU_C = 8.0
SSD_HEADS = 8
SSD_HEADDIM = 64
SSD_INNER = 512
SSD_GROUPS = 2
SSD_HPG = 4
SSD_STATE = 64
SSD_CONV_DIM = 768
DN_HEADS = 4
DN_DK = 64
DN_DV = 64
DN_CONV_DIM = 768
FFN_RES = 0.5
LN_EPS = 1e-5
RMS_EPS = 1e-6

CONV_W = LRU_WIDTH + SSD_CONV_DIM + DN_CONV_DIM
COL_LRU_Y = CONV_W
COL_SSD_Z = COL_LRU_Y + LRU_WIDTH
COL_DN_GATE = COL_SSD_Z + SSD_INNER
COL_SMALL = COL_DN_GATE + DN_HEADS * DN_DV
LANES = 128
SUBLANES = 8
PROJ_W = COL_SMALL + LANES
GATE_W = PROJ_W - CONV_W
SM_DT, SM_BETA, SM_ALPHA = 0, SSD_HEADS, SSD_HEADS + DN_HEADS
CHUNK = 64
V7X_VMEM_BYTES = 64 * 1024 * 1024


def _dot(a, b):
    return jnp.dot(a.astype(BF16), b.astype(BF16), preferred_element_type=F32)


def _dot_nt(a, b):
    return lax.dot_general(a.astype(BF16), b.astype(BF16), (((1,), (1,)), ((), ())), preferred_element_type=F32)


def _dot_tn(a, b):
    return lax.dot_general(a.astype(BF16), b.astype(BF16), (((0,), (0,)), ((), ())), preferred_element_type=F32)


def _segsum_lanes(x, ones_bd):
    hi = x.astype(BF16)
    lo = (x - hi.astype(F32)).astype(BF16)
    return (jnp.dot(hi, ones_bd, preferred_element_type=F32) + jnp.dot(lo, ones_bd, preferred_element_type=F32))


def _sigmoid(x):
    return 0.5 * jnp.tanh(0.5 * x) + 0.5


def _silu(x):
    hx = 0.5 * x
    return hx * jnp.tanh(hx) + hx


def _softplus(x):
    return jnp.maximum(x, 0.0) + jnp.log1p(jnp.exp(-jnp.abs(x)))


def _gelu_tanh(x):
    return 0.5 * x * (1.0 + jnp.tanh(0.7978845608028654 * (x + 0.044715 * (x * x * x))))


def _layer_norm(y, g, b):
    mu = jnp.mean(y, axis=-1, keepdims=True)
    yc = y - mu
    var = jnp.mean(yc * yc, axis=-1, keepdims=True)
    return yc * lax.rsqrt(var + LN_EPS) * g + b


def _iota2(shape, dim):
    return lax.broadcasted_iota(jnp.int32, shape, dim)


def _tile_rows(x, n):
    return jnp.concatenate([x] * n, axis=0)


def _expand_cols(x, c0, n):
    rows = x.shape[0]
    lane = _iota2((rows, LANES), 1)
    outs = []
    for j in range(n // 2):
        a = jnp.broadcast_to(x[:, c0 + 2 * j:c0 + 2 * j + 1], (rows, LANES))
        b = jnp.broadcast_to(x[:, c0 + 2 * j + 1:c0 + 2 * j + 2], (rows, LANES))
        outs.append(jnp.where(lane < 64, a, b))
    return jnp.concatenate(outs, axis=1)


def _chunk_cumsum(x):
    pos = _iota2(x.shape, 0) & (CHUNK - 1)
    s = 1
    while s < SUBLANES:
        x = x + jnp.where(pos >= s, pltpu.roll(x, s, 0), 0.0)
        s *= 2
    parts = [x[r0:r0 + CHUNK] for r0 in range(0, x.shape[0], CHUNK)]
    while s < CHUNK:
        parts = [jnp.concatenate([p[:s], p[s:] + p[:-s]], axis=0) for p in parts]
        s *= 2
    return jnp.concatenate(parts, axis=0)


def _ffn_ln_kernel(h_ref, wg_ref, wu_ref, wd_ref, lng_ref, lnb_ref, o_ref, act_ref, *, alpha, ff_chunk):
    h = h_ref[...]
    xb = h.astype(BF16)
    d_ff = wg_ref.shape[1]
    for c0 in range(0, d_ff, ff_chunk):
        g = jnp.dot(xb, wg_ref[:, c0:c0 + ff_chunk], preferred_element_type=F32)
        u = jnp.dot(xb, wu_ref[:, c0:c0 + ff_chunk], preferred_element_type=F32)
        act_ref[:, c0:c0 + ff_chunk] = (_silu(g) * u).astype(BF16)
    y = jnp.dot(act_ref[...], wd_ref[...], preferred_element_type=F32)
    o_ref[...] = _layer_norm(alpha * h + FFN_RES * y, lng_ref[...], lnb_ref[...])


def _resident(shape, lead=()):
    return pl.BlockSpec((None,) * len(lead) + tuple(shape), lambda *_: tuple(lead) + (0,) * len(shape),
                        pipeline_mode=pl.Buffered(1))


def _ffn_ln(h2d, wg, wu, wd, ln_g, ln_b, layer, which, *, alpha, tm):
    rows, d = h2d.shape
    d_ff = wg.shape[-1]
    lead = (layer, which)
    ff_chunk = 256
    weights = 3 * d * d_ff * 2
    tiles = 2 * 2 * tm * d * 4 + tm * d_ff * 2 + 4 * tm * ff_chunk * 4 + 2 * tm * d * 4
    vmem = min(weights + tiles + (8 << 20), V7X_VMEM_BYTES - (4 << 20))
    return pl.pallas_call(
        functools.partial(_ffn_ln_kernel, alpha=alpha, ff_chunk=ff_chunk),
        grid=(rows // tm,),
        in_specs=[
            pl.BlockSpec((tm, d), lambda i: (i, 0)),
            _resident((d, d_ff), lead), _resident((d, d_ff), lead), _resident((d_ff, d), lead),
            _resident((1, d), (layer, 2 * which)), _resident((1, d), (layer, 2 * which)),
        ],
        out_specs=pl.BlockSpec((tm, d), lambda i: (i, 0)),
        out_shape=jax.ShapeDtypeStruct((rows, d), F32),
        scratch_shapes=[pltpu.VMEM((tm, d_ff), BF16)],
        compiler_params=pltpu.CompilerParams(dimension_semantics=("arbitrary",), vmem_limit_bytes=vmem),
        name="ffn_ln",
    )(h2d, wg, wu, wd, ln_g, ln_b)


N_MIXER_IN = 13
N_MIXER_SCRATCH = 23
CONV_GROUPS = ((0, LRU_WIDTH), (LRU_WIDTH, LRU_WIDTH + SSD_CONV_DIM), (LRU_WIDTH + SSD_CONV_DIM, CONV_W))


def _mixer_ffn_kernel(*refs, alpha, rows, nb):
    streams = 1
    (h_ref, win_ref, cwb_ref, wgate_ref, gateb_ref, lam_ref, slabv_ref, ssdv_ref, dnv_ref,
     onesbd_ref, wout_ref, lng_ref, lnb_ref) = refs[:N_MIXER_IN]
    wg_ref, wu_ref, wd_ref, lng2_ref, lnb2_ref, o_ref = refs[N_MIXER_IN:N_MIXER_IN + 6]
    (proj_scr, slab_scr, xel_scr, xes_scr, xed_scr, la_scr, lb_scr, q_scr, k_scr, v_scr, be_scr, cg_scr,
     xs_scr, dte_scr, cse_scr, bm_scr, cm_scr, hl_scr, ys_scr, od_scr,
     lruh_scr, ssds_scr, dns_scr) = refs[N_MIXER_IN + 6:N_MIXER_IN + 6 + N_MIXER_SCRATCH]
    mid_scr, midb_scr, act_scr = refs[N_MIXER_IN + 6 + N_MIXER_SCRATCH:]

    step = pl.program_id(0)
    tb = lax.rem(jnp.minimum(step, pl.num_programs(0) - 2), nb)
    slot = lax.rem(step, 2)
    mid_w = mid_scr.at[slot]
    mid_r = mid_scr.at[1 - slot]
    midb_w = midb_scr.at[slot]
    midb_r = midb_scr.at[1 - slot]

    @pl.when(step == 0)
    def _():
        mid_scr[...] = jnp.zeros_like(mid_scr)
        midb_scr[...] = jnp.zeros_like(midb_scr)

    d_ff = wg_ref.shape[1]
    ff_chunk, out_chunk = 256, 256

    def act_unit(c0):
        def run():
            xb = midb_r[...]
            g = jnp.dot(xb, wg_ref[:, c0:c0 + ff_chunk], preferred_element_type=F32)
            u = jnp.dot(xb, wu_ref[:, c0:c0 + ff_chunk], preferred_element_type=F32)
            act_scr[:, c0:c0 + ff_chunk] = (_silu(g) * u).astype(BF16)
        return run

    def down_unit(n0):
        def run():
            y = jnp.dot(act_scr[...], wd_ref[:, n0:n0 + out_chunk], preferred_element_type=F32)
            o_ref[0, :, n0:n0 + out_chunk] = alpha * mid_r[:, n0:n0 + out_chunk] + FFN_RES * y
        return run

    def ln_unit():
        o_ref[0] = _layer_norm(o_ref[0], lng2_ref[...], lnb2_ref[...])

    ffn_units = ([act_unit(c0) for c0 in range(0, d_ff, ff_chunk)]
                 + [down_unit(n0) for n0 in range(0, wd_ref.shape[1], out_chunk)] + [ln_unit])

    def emit_ffn(n=1):
        for _ in range(n):
            if ffn_units:
                ffn_units.pop(0)()

    @pl.when(tb == 0)
    def _():
        for xe_g in (xel_scr, xes_scr, xed_scr):
            xe_g[:, 0:SUBLANES, :] = jnp.zeros((streams, SUBLANES, xe_g.shape[2]), F32)
        lruh_scr[...] = jnp.zeros_like(lruh_scr)
        ssds_scr[...] = jnp.zeros_like(ssds_scr)
        dns_scr[...] = jnp.zeros_like(dns_scr)

    ones_bd = onesbd_ref[...]
    nq = DN_HEADS * DN_DK


    def prepare(si):
        xes = (xel_scr.at[si], xes_scr.at[si], xed_scr.at[si])
        proj = proj_scr.at[si]
        slab = slab_scr.at[si]
        hb = h_ref[si].astype(BF16)

        def project_conv_cols(g):
            c0, c1 = CONV_GROUPS[g]
            xes[g][SUBLANES:SUBLANES + rows, :] = jnp.dot(hb, win_ref[:, c0:c1], preferred_element_type=F32)

        def conv_cols(g):
            c0, c1 = CONV_GROUPS[g]
            xe = xes[g]
            w = cwb_ref[:, c0:c1]
            y = xe[SUBLANES:SUBLANES + rows, :] * w[3:4] + w[4:5]
            for s in (1, 2, 3):
                y = y + xe[SUBLANES - s:SUBLANES - s + rows, :] * w[3 - s:4 - s]
            xe[0:SUBLANES, :] = xe[rows:rows + SUBLANES, :]
            return y

        project_conv_cols(0)
        project_conv_cols(1)
        project_conv_cols(2)
        slab[...] = jnp.dot(hb, win_ref[:, COL_SMALL:], preferred_element_type=F32)
        emit_ffn(3)

        u = conv_cols(0)
        gates = _dot(u, wgate_ref[...]) + gateb_ref[...]
        emit_ffn(1)
        r = _sigmoid(gates[:, :LRU_WIDTH])
        i_gate = _sigmoid(gates[:, LRU_WIDTH:])
        log_a = (-LRU_C) * r * _softplus(-lam_ref[...])
        a = jnp.exp(log_a)
        la_scr[si] = a
        lb_scr[si] = jnp.sqrt(1.0 - a * a) * (i_gate * u)

        xbc = _silu(conv_cols(1))
        xs_scr[si] = xbc[:, :SSD_INNER]
        bm_scr[si] = xbc[:, SSD_INNER:SSD_INNER + LANES]
        cm_scr[si] = xbc[:, SSD_INNER + LANES:]

        qkv = _silu(conv_cols(2))
        qh = qkv[:, :nq]
        kh = qkv[:, nq:2 * nq]
        q_scr[si] = qh * lax.rsqrt(_segsum_lanes(qh * qh, ones_bd) + RMS_EPS) * (DN_DK ** -0.5)
        k_scr[si] = kh * lax.rsqrt(_segsum_lanes(kh * kh, ones_bd) + RMS_EPS)
        v_scr[si] = qkv[:, 2 * nq:]
        emit_ffn(1)

        proj[...] = jnp.dot(hb, win_ref[:, CONV_W:COL_SMALL], preferred_element_type=F32)

        sl = slab[...]
        sv = slabv_ref[...]
        sp = _softplus(sl + sv[0:1])
        sg = _sigmoid(sl)
        coef = jnp.where(sv[2:3] > 0.0, -jnp.exp(sv[1:2]), 0.0)
        cs = _chunk_cumsum(sp * coef)
        dte_scr[si] = _expand_cols(sp, SM_DT, SSD_HEADS)
        cse_scr[si] = _expand_cols(cs, SM_DT, SSD_HEADS)
        be_scr[si] = _expand_cols(sg, SM_BETA, DN_HEADS)
        cg_scr[si] = _expand_cols(cs, SM_ALPHA, DN_HEADS)

    for si in range(streams):
        prepare(si)

    r4 = _iota2((CHUNK, 256), 0)
    c4 = _iota2((CHUNK, 256), 1) & (CHUNK - 1)
    eye4 = c4 == r4
    tril4 = c4 <= r4
    strict4 = c4 < r4
    eye4f = jnp.where(eye4, 1.0, 0.0)
    bd4_mask = (_iota2((256, 256), 0) >> 6) == (_iota2((256, 256), 1) >> 6)
    bd4 = bd4_mask.astype(F32)
    bd4_b = bd4_mask.astype(BF16)
    r8 = _iota2((CHUNK, SSD_INNER), 0)
    c8 = _iota2((CHUNK, SSD_INNER), 1) & (CHUNK - 1)
    eye8 = c8 == r8
    tril8 = c8 <= r8
    bd2_b = ((_iota2((LANES, LANES), 0) >> 6) == (_iota2((LANES, LANES), 1) >> 6)).astype(BF16)
    gmask = ((_iota2((LANES, SSD_INNER), 0) >> 6) == (_iota2((LANES, SSD_INNER), 1) >> 8)).astype(F32)
    lane_lo = _iota2((CHUNK, LANES), 1) < 64
    pos = _iota2((CHUNK, LRU_WIDTH), 0)

    def bdiag4(x):
        return _tile_rows(x.astype(BF16), 4) * bd4_b

    nch = rows // CHUNK
    items = [(si, slice(c * CHUNK, (c + 1) * CHUNK)) for si in range(streams) for c in range(nch)]

    dn = []
    for si, rs in items:
        q = q_scr[si, rs, :]
        k = k_scr[si, rs, :]
        be = be_scr[si, rs, :]
        cg = cg_scr[si, rs, :]
        eg = jnp.exp(cg)
        kb = k * be
        vb = v_scr[si, rs, :] * be
        rowg = jnp.sum(jnp.where(eye4, cg, 0.0), axis=0, keepdims=True)
        dm = jnp.where(tril4, jnp.exp(jnp.minimum(cg - rowg, 0.0)), 0.0)
        a_all = _dot_nt(jnp.concatenate([q, kb], axis=0), bdiag4(k))
        lastg = cg[CHUNK - 1:CHUNK, :]
        dn.append(dict(attn=a_all[:CHUNK] * dm, m=jnp.where(strict4, a_all[CHUNK:] * dm, 0.0),
                       rhs=jnp.concatenate([bdiag4(vb), bdiag4(kb * eg)], axis=1), qe=q * eg,
                       k_dec=k * jnp.exp(lastg - cg), dec=jnp.exp(lastg)))
    emit_ffn(1)
    ps = [-d["m"] for d in dn]
    bps = [bdiag4(p) for p in ps]
    ts = [eye4f + p for p in ps]
    ps = [_dot(p, bp) for p, bp in zip(ps, bps)]
    emit_ffn(1)
    for j in range(4):
        bps = [bdiag4(p) for p in ps]
        pt = [_dot(jnp.concatenate([p, t], axis=0), bp) for p, t, bp in zip(ps, ts, bps)]
        ps = [x[:CHUNK] for x in pt]
        ts = [t + x[CHUNK:] for t, x in zip(ts, pt)]
        emit_ffn(j % 2)
    ts = [t + _dot(t, bdiag4(p)) for t, p in zip(ts, ps)]
    uws = [_dot(t, d["rhs"]) for t, d in zip(ts, dn)]
    emit_ffn(1)

    ssd = []
    for si, rs in items:
        bm = bm_scr[si, rs, :]
        cm = cm_scr[si, rs, :]
        cse = cse_scr[si, rs, :]
        xdt = xs_scr[si, rs, :] * dte_scr[si, rs, :]
        row8 = jnp.sum(jnp.where(eye8, cse, 0.0), axis=0, keepdims=True)
        lmat = jnp.where(tril8, jnp.exp(jnp.minimum(cse - row8, 0.0)), 0.0)
        g2 = _dot_nt(cm, _tile_rows(bm.astype(BF16), 2) * bd2_b)
        g2r = pltpu.roll(g2, 64, 1)
        gg0 = jnp.where(lane_lo, g2, g2r)
        gg1 = jnp.where(lane_lo, g2r, g2)
        mx = jnp.concatenate([gg0, gg0, gg1, gg1], axis=1) * lmat
        yd = jnp.concatenate([_dot(mx[:, :256], bdiag4(xdt[:, :256])),
                              _dot(mx[:, 256:], bdiag4(xdt[:, 256:]))], axis=1)
        last8 = cse[CHUNK - 1:CHUNK, :]
        ssd.append(dict(yd=yd, cm=cm, ecs=jnp.exp(cse), dec=jnp.exp(last8),
                        upd=_dot_tn(bm, xdt * jnp.exp(last8 - cse)) * gmask))

    emit_ffn(1)

    lru = []
    for si, rs in items:
        a_c = la_scr[si, rs, :]
        b_c = lb_scr[si, rs, :]
        s = 1
        while s < SUBLANES:
            keep = pos >= s
            a_sh = jnp.where(keep, pltpu.roll(a_c, s, 0), 1.0)
            b_sh = jnp.where(keep, pltpu.roll(b_c, s, 0), 0.0)
            b_c = a_c * b_sh + b_c
            a_c = a_c * a_sh
            s *= 2
        while s < CHUNK:
            b_c = jnp.concatenate([b_c[:s], a_c[s:] * b_c[:-s] + b_c[s:]], axis=0)
            a_c = jnp.concatenate([a_c[:s], a_c[s:] * a_c[:-s]], axis=0)
            s *= 2
        lru.append((a_c, b_c))

    lru_h = [lruh_scr[si, 0:1, :] for si in range(streams)]
    s_ssd = [ssds_scr[si] for si in range(streams)]
    s_dn = [dns_scr[si] for si in range(streams)]
    for c in range(nch):
        for si in range(streams):
            i = si * nch + c
            rs = items[i][1]
            hseq = lru[i][0] * lru_h[si] + lru[i][1]
            hl_scr[si, rs, :] = hseq
            lru_h[si] = hseq[CHUNK - 1:CHUNK, :]

            sd = ssd[i]
            ys_scr[si, rs, :] = sd["yd"] + _dot(sd["cm"], s_ssd[si]) * sd["ecs"]
            s_ssd[si] = s_ssd[si] * sd["dec"] + sd["upd"]

            d = dn[i]
            ws = _dot(jnp.concatenate([uws[i][:, 256:], d["qe"]], axis=0), s_dn[si])
            v_new = uws[i][:, :256] - ws[:CHUNK]
            od_scr[si, rs, :] = ws[CHUNK:] + _dot(d["attn"], bdiag4(v_new))
            s_dn[si] = s_dn[si] * d["dec"] + _dot_tn(d["k_dec"], v_new) * bd4
        if c < nch - 1:
            emit_ffn(1)

    for si in range(streams):
        lruh_scr[si, 0:1, :] = lru_h[si]
        ssds_scr[si] = s_ssd[si]
        dns_scr[si] = s_dn[si]

    ssdv = ssdv_ref[...]
    gw = SSD_HPG * SSD_HEADDIM
    for si in range(streams):
        proj = proj_scr.at[si]
        o_lru = hl_scr[si] * _gelu_tanh(proj[:, COL_LRU_Y - CONV_W:COL_SSD_Z - CONV_W])
        y = ys_scr[si] + ssdv[0:1] * xs_scr[si]
        y = y * _silu(proj[:, COL_SSD_Z - CONV_W:COL_DN_GATE - CONV_W])
        o_ssd = []
        for g in range(SSD_GROUPS):
            yg = y[:, g * gw:(g + 1) * gw]
            ms = jnp.mean(yg * yg, axis=-1, keepdims=True)
            o_ssd.append(yg * lax.rsqrt(ms + RMS_EPS) * ssdv[1:2, g * gw:(g + 1) * gw])
        od = od_scr[si]
        ms = _segsum_lanes(od * od, ones_bd) * (1.0 / DN_DV)
        o_dn = od * lax.rsqrt(ms + RMS_EPS) * dnv_ref[0:1] * _silu(proj[:, COL_DN_GATE - CONV_W:COL_SMALL - CONV_W])
        mix = jnp.concatenate([o_lru] + o_ssd + [o_dn], axis=1)
        y2 = alpha * h_ref[si] + _dot(mix, wout_ref[...])
        m_out = _layer_norm(y2, lng_ref[...], lnb_ref[...])
        mid_w[...] = m_out
        midb_w[...] = m_out.astype(BF16)
    emit_ffn(len(ffn_units))


def _mixer_scratch(rows, streams):
    rf = lambda *s: pltpu.VMEM((streams,) + s, F32)
    return [
        rf(rows, COL_SMALL - CONV_W), rf(rows, LANES),
        rf(rows + SUBLANES, LRU_WIDTH), rf(rows + SUBLANES, SSD_CONV_DIM), rf(rows + SUBLANES, DN_CONV_DIM),
        rf(rows, 256), rf(rows, 256),
        rf(rows, 256), rf(rows, 256), rf(rows, 256), rf(rows, 256), rf(rows, 256),
        rf(rows, SSD_INNER), rf(rows, SSD_INNER), rf(rows, SSD_INNER),
        rf(rows, LANES), rf(rows, LANES),
        rf(rows, 256), rf(rows, SSD_INNER), rf(rows, 256),
        rf(SUBLANES, 256), rf(LANES, SSD_INNER), rf(256, 256),
    ]


def _mixer_ffn(h3d, p, wg, wu, wd, ln_g, ln_b, layer, *, alpha, rows):
    b, lp, d = h3d.shape
    nb = lp // rows
    nblk = b * nb
    d_ff = wg.shape[-1]
    one = (layer,)
    param_specs = [
        _resident((d, PROJ_W), one), _resident((SUBLANES, CONV_W), one), _resident((LRU_WIDTH, 2 * LRU_WIDTH), one),
        _resident((1, 2 * LRU_WIDTH), one), _resident((1, LRU_WIDTH), one), _resident((SUBLANES, LANES), one),
        _resident((SUBLANES, SSD_INNER), one), _resident((SUBLANES, 256), one), _resident((256, 256)),
        _resident((d, d), one), _resident((1, d), (layer, 1)), _resident((1, d), (layer, 1)),
        _resident((d, d_ff), (layer, 1)), _resident((d, d_ff), (layer, 1)), _resident((d_ff, d), (layer, 1)),
        _resident((1, d), (layer, 2)), _resident((1, d), (layer, 2)),
    ]
    param_args = (p["win"], p["cwb"], p["wgate"], p["gateb"], p["lam"], p["slabv"], p["ssdv"], p["dnv"],
                  p["ones_bd"], p["wout"], ln_g, ln_b, wg, wu, wd, ln_g, ln_b)
    f32_row_words = GATE_W + CONV_W + 8 * 256 + 4 * SSD_INNER + 2 * LANES
    scratch = (rows + SUBLANES) * f32_row_words * 4 + 2 * rows * d * 4 + rows * d_ff * 2
    blocks = 2 * 2 * rows * d * 4
    weights = (d * PROJ_W + d * d + 256 * 512 + 256 * 256 + 3 * d * d_ff) * 2
    vmem = min(scratch + blocks + weights + (12 << 20), V7X_VMEM_BYTES - (4 << 20))

    def in_block(s):
        m = jnp.minimum(s, nblk - 1)
        return (m // nb, m % nb, 0)

    def out_block(s):
        m = jnp.maximum(s - 1, 0)
        return (m // nb, m % nb, 0)

    return pl.pallas_call(
        functools.partial(_mixer_ffn_kernel, alpha=alpha, rows=rows, nb=nb),
        grid=(nblk + 1,),
        in_specs=[pl.BlockSpec((1, rows, d), in_block)] + param_specs,
        out_specs=pl.BlockSpec((1, rows, d), out_block),
        out_shape=jax.ShapeDtypeStruct((b, lp, d), F32),
        scratch_shapes=_mixer_scratch(rows, 1) + [
            pltpu.VMEM((2, rows, d), F32), pltpu.VMEM((2, rows, d), BF16), pltpu.VMEM((rows, d_ff), BF16)],
        compiler_params=pltpu.CompilerParams(dimension_semantics=("arbitrary",), vmem_limit_bytes=vmem),
        name="mixer_ffn",
    )(h3d, *param_args)


def _mixer_params(w_in, lru_conv_w, lru_conv_b, lru_w_a, lru_b_a, lru_w_x, lru_b_x, lru_lambda,
                  ssd_conv_w, ssd_conv_b, ssd_dt_bias, ssd_a_log, ssd_d, ssd_norm_w,
                  dn_conv_w, dn_a_log, dn_dt_bias, dn_norm_w, w_out):
    depth, d, _ = w_in.shape
    o = 0
    parts = {}
    for name, size in (("lru_u", LRU_WIDTH), ("lru_y", LRU_WIDTH), ("ssd_z", SSD_INNER), ("ssd_xbc", SSD_CONV_DIM),
                       ("ssd_dt", SSD_HEADS), ("dn_qkv", DN_CONV_DIM), ("dn_gate", DN_HEADS * DN_DV),
                       ("dn_beta", DN_HEADS), ("dn_alpha", DN_HEADS)):
        parts[name] = w_in[:, :, o:o + size]
        o += size
    zeros = lambda *s: jnp.zeros((depth,) + s, F32)
    win = jnp.concatenate([parts["lru_u"], parts["ssd_xbc"], parts["dn_qkv"], parts["lru_y"], parts["ssd_z"],
                           parts["dn_gate"], parts["ssd_dt"], parts["dn_beta"], parts["dn_alpha"],
                           zeros(d, PROJ_W - COL_SMALL - SSD_HEADS - 2 * DN_HEADS)], axis=2).astype(BF16)
    conv_w = jnp.concatenate([lru_conv_w, ssd_conv_w, dn_conv_w], axis=2)
    conv_b = jnp.concatenate([lru_conv_b, ssd_conv_b, zeros(DN_CONV_DIM)], axis=1)
    cwb = jnp.concatenate([conv_w, conv_b[:, None], zeros(SUBLANES - CONV_K - 1, CONV_W)], axis=1)
    eye = jnp.eye(LRU_HEADS, dtype=F32)
    bdiag = lambda w: jnp.einsum("lhij,hg->lhigj", w, eye).reshape(depth, LRU_WIDTH, LRU_WIDTH)
    wgate = jnp.concatenate([bdiag(lru_w_a), bdiag(lru_w_x)], axis=2).astype(BF16)
    gap = zeros(SM_ALPHA - SM_DT - SSD_HEADS)
    tail = zeros(LANES - SM_ALPHA - DN_HEADS)
    used = jnp.concatenate([jnp.ones((depth, SSD_HEADS), F32), gap, jnp.ones((depth, DN_HEADS), F32), tail], axis=1)
    slab = jnp.stack([jnp.concatenate([ssd_dt_bias, gap, dn_dt_bias, tail], axis=1),
                      jnp.concatenate([ssd_a_log, gap, dn_a_log, tail], axis=1), used]
                     + [zeros(LANES)] * (SUBLANES - 3), axis=1)
    ssdv = jnp.stack([jnp.repeat(ssd_d, SSD_HEADDIM, axis=1), ssd_norm_w] + [zeros(SSD_INNER)] * (SUBLANES - 2),
                     axis=1)
    dnv = jnp.stack([jnp.tile(dn_norm_w, (1, DN_HEADS))] + [zeros(DN_HEADS * DN_DV)] * (SUBLANES - 1), axis=1)
    seg = jnp.arange(256) // 64
    return dict(
        win=win, cwb=cwb, wgate=wgate,
        gateb=jnp.concatenate([lru_b_a, lru_b_x], axis=1)[:, None], lam=lru_lambda[:, None],
        slabv=slab, ssdv=ssdv, dnv=dnv, ones_bd=(seg[:, None] == seg[None, :]).astype(BF16),
        wout=w_out.astype(BF16),
    )


def _plan(batch, length):
    rows = min((320, 256, 128, 64), key=lambda r: (-(-length // r) * r, -r))
    lp = -(-length // rows) * rows
    tm = next(t for t in (512, 256, 128, 64, 32, 16, 8) if (batch * lp) % t == 0)
    return rows, lp, tm


def _forward(x, meta, ln_g, ln_b, ffn_w_gate, ffn_w_up, ffn_w_down, w_in, lru_conv_w, lru_conv_b, lru_w_a, lru_b_a,
             lru_w_x, lru_b_x, lru_lambda, ssd_conv_w, ssd_conv_b, ssd_dt_bias, ssd_a_log, ssd_d, ssd_norm_w,
             dn_conv_w, dn_a_log, dn_dt_bias, dn_norm_w, w_out, plan=None):
    bsz, seq, d = x.shape
    depth = ln_g.shape[0]
    alpha = float((2 * depth) ** 0.25)
    length = N_META + seq
    rows, lp, tm = plan or _plan(bsz, length)
    h = jnp.concatenate([jnp.broadcast_to(meta.astype(x.dtype)[None], (bsz, N_META, d)), x,
                         jnp.zeros((bsz, lp - length, d), x.dtype)], axis=1)
    wg = ffn_w_gate.astype(BF16)
    wu = ffn_w_up.astype(BF16)
    wd = ffn_w_down.astype(BF16)
    lng = ln_g[:, :, None, :]
    lnb = ln_b[:, :, None, :]
    mp = _mixer_params(w_in, lru_conv_w, lru_conv_b, lru_w_a, lru_b_a, lru_w_x, lru_b_x, lru_lambda,
                       ssd_conv_w, ssd_conv_b, ssd_dt_bias, ssd_a_log, ssd_d, ssd_norm_w,
                       dn_conv_w, dn_a_log, dn_dt_bias, dn_norm_w, w_out)
    for l in range(depth):
        h = _ffn_ln(h.reshape(bsz * lp, d), wg, wu, wd, lng, lnb, l, 0, alpha=alpha, tm=tm).reshape(bsz, lp, d)
        h = _mixer_ffn(h, mp, wg, wu, wd, lng, lnb, l, alpha=alpha, rows=rows)
    return h[:, N_META:length]


def kernel(x, meta, ln_g, ln_b, ffn_w_gate, ffn_w_up, ffn_w_down, w_in, lru_conv_w, lru_conv_b, lru_w_a, lru_b_a,
           lru_w_x, lru_b_x, lru_lambda, ssd_conv_w, ssd_conv_b, ssd_dt_bias, ssd_a_log, ssd_d, ssd_norm_w,
           dn_conv_w, dn_a_log, dn_dt_bias, dn_norm_w, w_out):
    return _forward(x, meta, ln_g, ln_b, ffn_w_gate, ffn_w_up, ffn_w_down, w_in, lru_conv_w, lru_conv_b, lru_w_a,
                    lru_b_a, lru_w_x, lru_b_x, lru_lambda, ssd_conv_w, ssd_conv_b, ssd_dt_bias, ssd_a_log, ssd_d,
                    ssd_norm_w, dn_conv_w, dn_a_log, dn_dt_bias, dn_norm_w, w_out)
```

```python
import functools

import jax
import jax.numpy as jnp
from jax import lax
from jax.experimental import pallas as pl
from jax.experimental.pallas import tpu as pltpu

F32 = jnp.float32
BF16 = jnp.bfloat16

D_MODEL = 1024
N_META = 16
CONV_K = 4
D_FF = 2816
LRU_WIDTH = 256
LRU_HEADS = 4
LRU_BLOCK = 64
LRU_C = 8.0
SSD_HEADS = 8
SSD_HEADDIM = 64
SSD_INNER = 512
SSD_GROUPS = 2
SSD_HPG = 4
SSD_STATE = 64
SSD_CONV_DIM = 768
DN_HEADS = 4
DN_DK = 64
DN_DV = 64
DN_CONV_DIM = 768
FFN_RES = 0.5
LN_EPS = 1e-5
RMS_EPS = 1e-6

CONV_W = LRU_WIDTH + SSD_CONV_DIM + DN_CONV_DIM
COL_LRU_Y = CONV_W
COL_SSD_Z = COL_LRU_Y + LRU_WIDTH
COL_DN_GATE = COL_SSD_Z + SSD_INNER
COL_SMALL = COL_DN_GATE + DN_HEADS * DN_DV
LANES = 128
SUBLANES = 8
PROJ_W = COL_SMALL + LANES
GATE_W = PROJ_W - CONV_W
SM_DT, SM_BETA, SM_ALPHA = 0, SSD_HEADS, SSD_HEADS + DN_HEADS
CHUNK = 64
V7X_VMEM_BYTES = 64 * 1024 * 1024


def _dot(a, b):
    return jnp.dot(a.astype(BF16), b.astype(BF16), preferred_element_type=F32)


def _dot_nt(a, b):
    return lax.dot_general(a.astype(BF16), b.astype(BF16), (((1,), (1,)), ((), ())), preferred_element_type=F32)


def _dot_tn(a, b):
    return lax.dot_general(a.astype(BF16), b.astype(BF16), (((0,), (0,)), ((), ())), preferred_element_type=F32)


def _segsum_lanes(x, ones_bd):
    hi = x.astype(BF16)
    lo = (x - hi.astype(F32)).astype(BF16)
    return (jnp.dot(hi, ones_bd, preferred_element_type=F32) + jnp.dot(lo, ones_bd, preferred_element_type=F32))


def _sigmoid(x):
    return 0.5 * jnp.tanh(0.5 * x) + 0.5


def _silu(x):
    hx = 0.5 * x
    return hx * jnp.tanh(hx) + hx


def _softplus(x):
    return jnp.maximum(x, 0.0) + jnp.log1p(jnp.exp(-jnp.abs(x)))


def _gelu_tanh(x):
    return 0.5 * x * (1.0 + jnp.tanh(0.7978845608028654 * (x + 0.044715 * (x * x * x))))


def _layer_norm(y, g, b):
    mu = jnp.mean(y, axis=-1, keepdims=True)
    yc = y - mu
    var = jnp.mean(yc * yc, axis=-1, keepdims=True)
    return yc * lax.rsqrt(var + LN_EPS) * g + b


def _iota2(shape, dim):
    return lax.broadcasted_iota(jnp.int32, shape, dim)


def _tile_rows(x, n):
    return jnp.concatenate([x] * n, axis=0)


def _expand_cols(x, c0, n):
    rows = x.shape[0]
    lane = _iota2((rows, LANES), 1)
    outs = []
    for j in range(n // 2):
        a = jnp.broadcast_to(x[:, c0 + 2 * j:c0 + 2 * j + 1], (rows, LANES))
        b = jnp.broadcast_to(x[:, c0 + 2 * j + 1:c0 + 2 * j + 2], (rows, LANES))
        outs.append(jnp.where(lane < 64, a, b))
    return jnp.concatenate(outs, axis=1)


def _chunk_cumsum(x):
    pos = _iota2(x.shape, 0) & (CHUNK - 1)
    s = 1
    while s < SUBLANES:
        x = x + jnp.where(pos >= s, pltpu.roll(x, s, 0), 0.0)
        s *= 2
    parts = [x[r0:r0 + CHUNK] for r0 in range(0, x.shape[0], CHUNK)]
    while s < CHUNK:
        parts = [jnp.concatenate([p[:s], p[s:] + p[:-s]], axis=0) for p in parts]
        s *= 2
    return jnp.concatenate(parts, axis=0)


def _ffn_ln_kernel(h_ref, wg_ref, wu_ref, wd_ref, lng_ref, lnb_ref, o_ref, act_ref, *, alpha, ff_chunk, parts):
    d_ff = wg_ref.shape[1]
    for r0, r1 in parts:
        h = h_ref[r0:r1, :]
        xb = h.astype(BF16)
        for c0 in range(0, d_ff, ff_chunk):
            g = jnp.dot(xb, wg_ref[:, c0:c0 + ff_chunk], preferred_element_type=F32)
            u = jnp.dot(xb, wu_ref[:, c0:c0 + ff_chunk], preferred_element_type=F32)
            act_ref[r0:r1, c0:c0 + ff_chunk] = (_silu(g) * u).astype(BF16)
        y = jnp.dot(act_ref[r0:r1, :], wd_ref[...], preferred_element_type=F32)
        o_ref[r0:r1, :] = _layer_norm(alpha * h + FFN_RES * y, lng_ref[...], lnb_ref[...])


def _resident(shape, lead=()):
    return pl.BlockSpec((None,) * len(lead) + tuple(shape), lambda *_: tuple(lead) + (0,) * len(shape),
                        pipeline_mode=pl.Buffered(1))


def _ffn_ln(h2d, wg, wu, wd, ln_g, ln_b, layer, which, *, alpha, tm):
    rows, d = h2d.shape
    d_ff = wg.shape[-1]
    lead = (layer, which)
    ff_chunk = 256
    weights = 3 * d * d_ff * 2
    tiles = 2 * 2 * tm * d * 4 + tm * d_ff * 2 + 4 * tm * ff_chunk * 4 + 2 * tm * d * 4
    vmem = min(weights + tiles + (8 << 20), V7X_VMEM_BYTES - (4 << 20))
    half = tm // 2 // 16 * 16
    parts = ((0, half), (half, tm)) if half >= 256 else ((0, tm),)
    return pl.pallas_call(
        functools.partial(_ffn_ln_kernel, alpha=alpha, ff_chunk=ff_chunk, parts=parts),
        grid=(rows // tm,),
        in_specs=[
            pl.BlockSpec((tm, d), lambda i: (i, 0)),
            _resident((d, d_ff), lead), _resident((d, d_ff), lead), _resident((d_ff, d), lead),
            _resident((1, d), (layer, 2 * which)), _resident((1, d), (layer, 2 * which)),
        ],
        out_specs=pl.BlockSpec((tm, d), lambda i: (i, 0)),
        out_shape=jax.ShapeDtypeStruct((rows, d), F32),
        scratch_shapes=[pltpu.VMEM((tm, d_ff), BF16)],
        compiler_params=pltpu.CompilerParams(dimension_semantics=("arbitrary",), vmem_limit_bytes=vmem),
        name="ffn_ln",
    )(h2d, wg, wu, wd, ln_g, ln_b)


N_MIXER_IN = 13
N_MIXER_SCRATCH = 23
CONV_GROUPS = ((0, LRU_WIDTH), (LRU_WIDTH, LRU_WIDTH + SSD_CONV_DIM), (LRU_WIDTH + SSD_CONV_DIM, CONV_W))


def _mixer_ffn_kernel(*refs, alpha, rows, nb):
    streams = 1
    (h_ref, win_ref, cwb_ref, wgate_ref, gateb_ref, lam_ref, slabv_ref, ssdv_ref, dnv_ref,
     onesbd_ref, wout_ref, lng_ref, lnb_ref) = refs[:N_MIXER_IN]
    wg_ref, wu_ref, wd_ref, lng2_ref, lnb2_ref, o_ref = refs[N_MIXER_IN:N_MIXER_IN + 6]
    (proj_scr, slab_scr, xel_scr, xes_scr, xed_scr, la_scr, lb_scr, q_scr, k_scr, v_scr, be_scr, cg_scr,
     xs_scr, dte_scr, cse_scr, bm_scr, cm_scr, hl_scr, ys_scr, od_scr,
     lruh_scr, ssds_scr, dns_scr) = refs[N_MIXER_IN + 6:N_MIXER_IN + 6 + N_MIXER_SCRATCH]
    mid_scr, midb_scr, act_scr = refs[N_MIXER_IN + 6 + N_MIXER_SCRATCH:]

    step = pl.program_id(0)
    tb = lax.rem(jnp.minimum(step, pl.num_programs(0) - 2), nb)
    slot = lax.rem(step, 2)
    mid_w = mid_scr.at[slot]
    mid_r = mid_scr.at[1 - slot]
    midb_w = midb_scr.at[slot]
    midb_r = midb_scr.at[1 - slot]

    @pl.when(step == 0)
    def _():
        mid_scr[...] = jnp.zeros_like(mid_scr)
        midb_scr[...] = jnp.zeros_like(midb_scr)

    d_ff = wg_ref.shape[1]
    ff_chunk, out_chunk = 256, 256

    def act_unit(c0):
        def run():
            xb = midb_r[...]
            g = jnp.dot(xb, wg_ref[:, c0:c0 + ff_chunk], preferred_element_type=F32)
            u = jnp.dot(xb, wu_ref[:, c0:c0 + ff_chunk], preferred_element_type=F32)
            act_scr[:, c0:c0 + ff_chunk] = (_silu(g) * u).astype(BF16)
        return run

    def down_unit(n0):
        def run():
            y = jnp.dot(act_scr[...], wd_ref[:, n0:n0 + out_chunk], preferred_element_type=F32)
            o_ref[0, :, n0:n0 + out_chunk] = alpha * mid_r[:, n0:n0 + out_chunk] + FFN_RES * y
        return run

    def ln_unit():
        o_ref[0] = _layer_norm(o_ref[0], lng2_ref[...], lnb2_ref[...])

    ffn_units = ([act_unit(c0) for c0 in range(0, d_ff, ff_chunk)]
                 + [down_unit(n0) for n0 in range(0, wd_ref.shape[1], out_chunk)] + [ln_unit])

    def emit_ffn(n=1):
        for _ in range(n):
            if ffn_units:
                ffn_units.pop(0)()

    @pl.when(tb == 0)
    def _():
        for xe_g in (xel_scr, xes_scr, xed_scr):
            xe_g[:, 0:SUBLANES, :] = jnp.zeros((streams, SUBLANES, xe_g.shape[2]), F32)
        lruh_scr[...] = jnp.zeros_like(lruh_scr)
        ssds_scr[...] = jnp.zeros_like(ssds_scr)
        dns_scr[...] = jnp.zeros_like(dns_scr)

    ones_bd = onesbd_ref[...]
    nq = DN_HEADS * DN_DK


    def prepare(si):
        xes = (xel_scr.at[si], xes_scr.at[si], xed_scr.at[si])
        proj = proj_scr.at[si]
        slab = slab_scr.at[si]
        hb = h_ref[si].astype(BF16)

        def project_conv_cols(g):
            c0, c1 = CONV_GROUPS[g]
            xes[g][SUBLANES:SUBLANES + rows, :] = jnp.dot(hb, win_ref[:, c0:c1], preferred_element_type=F32)

        def conv_cols(g):
            c0, c1 = CONV_GROUPS[g]
            xe = xes[g]
            w = cwb_ref[:, c0:c1]
            y = xe[SUBLANES:SUBLANES + rows, :] * w[3:4] + w[4:5]
            for s in (1, 2, 3):
                y = y + xe[SUBLANES - s:SUBLANES - s + rows, :] * w[3 - s:4 - s]
            xe[0:SUBLANES, :] = xe[rows:rows + SUBLANES, :]
            return y

        project_conv_cols(0)
        project_conv_cols(1)
        project_conv_cols(2)
        slab[...] = jnp.dot(hb, win_ref[:, COL_SMALL:], preferred_element_type=F32)
        emit_ffn(3)

        u = conv_cols(0)
        gates = _dot(u, wgate_ref[...]) + gateb_ref[...]
        emit_ffn(1)
        r = _sigmoid(gates[:, :LRU_WIDTH])
        i_gate = _sigmoid(gates[:, LRU_WIDTH:])
        log_a = (-LRU_C) * r * _softplus(-lam_ref[...])
        a = jnp.exp(log_a)
        la_scr[si] = a
        lb_scr[si] = jnp.sqrt(1.0 - a * a) * (i_gate * u)

        xbc = _silu(conv_cols(1))
        xs_scr[si] = xbc[:, :SSD_INNER]
        bm_scr[si] = xbc[:, SSD_INNER:SSD_INNER + LANES]
        cm_scr[si] = xbc[:, SSD_INNER + LANES:]

        qkv = _silu(conv_cols(2))
        qh = qkv[:, :nq]
        kh = qkv[:, nq:2 * nq]
        q_scr[si] = qh * lax.rsqrt(_segsum_lanes(qh * qh, ones_bd) + RMS_EPS) * (DN_DK ** -0.5)
        k_scr[si] = kh * lax.rsqrt(_segsum_lanes(kh * kh, ones_bd) + RMS_EPS)
        v_scr[si] = qkv[:, 2 * nq:]
        emit_ffn(1)

        proj[...] = jnp.dot(hb, win_ref[:, CONV_W:COL_SMALL], preferred_element_type=F32)

        sl = slab[...]
        sv = slabv_ref[...]
        sp = _softplus(sl + sv[0:1])
        sg = _sigmoid(sl)
        coef = jnp.where(sv[2:3] > 0.0, -jnp.exp(sv[1:2]), 0.0)
        cs = _chunk_cumsum(sp * coef)
        dte_scr[si] = _expand_cols(sp, SM_DT, SSD_HEADS)
        cse_scr[si] = _expand_cols(cs, SM_DT, SSD_HEADS)
        be_scr[si] = _expand_cols(sg, SM_BETA, DN_HEADS)
        cg_scr[si] = _expand_cols(cs, SM_ALPHA, DN_HEADS)

    for si in range(streams):
        prepare(si)

    r4 = _iota2((CHUNK, 256), 0)
    c4 = _iota2((CHUNK, 256), 1) & (CHUNK - 1)
    eye4 = c4 == r4
    tril4 = c4 <= r4
    strict4 = c4 < r4
    eye4f = jnp.where(eye4, 1.0, 0.0)
    bd4_mask = (_iota2((256, 256), 0) >> 6) == (_iota2((256, 256), 1) >> 6)
    bd4 = bd4_mask.astype(F32)
    bd4_b = bd4_mask.astype(BF16)
    r8 = _iota2((CHUNK, SSD_INNER), 0)
    c8 = _iota2((CHUNK, SSD_INNER), 1) & (CHUNK - 1)
    eye8 = c8 == r8
    tril8 = c8 <= r8
    bd2_b = ((_iota2((LANES, LANES), 0) >> 6) == (_iota2((LANES, LANES), 1) >> 6)).astype(BF16)
    gmask = ((_iota2((LANES, SSD_INNER), 0) >> 6) == (_iota2((LANES, SSD_INNER), 1) >> 8)).astype(F32)
    lane_lo = _iota2((CHUNK, LANES), 1) < 64
    pos = _iota2((CHUNK, LRU_WIDTH), 0)

    def bdiag4(x):
        return _tile_rows(x.astype(BF16), 4) * bd4_b

    nch = rows // CHUNK
    items = [(si, slice(c * CHUNK, (c + 1) * CHUNK)) for si in range(streams) for c in range(nch)]

    dn = []
    for si, rs in items:
        q = q_scr[si, rs, :]
        k = k_scr[si, rs, :]
        be = be_scr[si, rs, :]
        cg = cg_scr[si, rs, :]
        eg = jnp.exp(cg)
        kb = k * be
        vb = v_scr[si, rs, :] * be
        rowg = jnp.sum(jnp.where(eye4, cg, 0.0), axis=0, keepdims=True)
        dm = jnp.where(tril4, jnp.exp(jnp.minimum(cg - rowg, 0.0)), 0.0)
        a_all = _dot_nt(jnp.concatenate([q, kb], axis=0), bdiag4(k))
        lastg = cg[CHUNK - 1:CHUNK, :]
        dn.append(dict(attn=a_all[:CHUNK] * dm, m=jnp.where(strict4, a_all[CHUNK:] * dm, 0.0),
                       rhs=jnp.concatenate([bdiag4(vb), bdiag4(kb * eg)], axis=1), qe=q * eg,
                       k_dec=k * jnp.exp(lastg - cg), dec=jnp.exp(lastg)))
    emit_ffn(1)
    ps = [-d["m"] for d in dn]
    bps = [bdiag4(p) for p in ps]
    ts = [eye4f + p for p in ps]
    ps = [_dot(p, bp) for p, bp in zip(ps, bps)]
    emit_ffn(1)
    for j in range(4):
        bps = [bdiag4(p) for p in ps]
        pt = [_dot(jnp.concatenate([p, t], axis=0), bp) for p, t, bp in zip(ps, ts, bps)]
        ps = [x[:CHUNK] for x in pt]
        ts = [t + x[CHUNK:] for t, x in zip(ts, pt)]
        emit_ffn(j % 2)
    ts = [t + _dot(t, bdiag4(p)) for t, p in zip(ts, ps)]
    uws = [_dot(t, d["rhs"]) for t, d in zip(ts, dn)]
    emit_ffn(1)

    ssd = []
    for si, rs in items:
        bm = bm_scr[si, rs, :]
        cm = cm_scr[si, rs, :]
        cse = cse_scr[si, rs, :]
        xdt = xs_scr[si, rs, :] * dte_scr[si, rs, :]
        row8 = jnp.sum(jnp.where(eye8, cse, 0.0), axis=0, keepdims=True)
        lmat = jnp.where(tril8, jnp.exp(jnp.minimum(cse - row8, 0.0)), 0.0)
        g2 = _dot_nt(cm, _tile_rows(bm.astype(BF16), 2) * bd2_b)
        g2r = pltpu.roll(g2, 64, 1)
        gg0 = jnp.where(lane_lo, g2, g2r)
        gg1 = jnp.where(lane_lo, g2r, g2)
        mx = jnp.concatenate([gg0, gg0, gg1, gg1], axis=1) * lmat
        yd = jnp.concatenate([_dot(mx[:, :256], bdiag4(xdt[:, :256])),
                              _dot(mx[:, 256:], bdiag4(xdt[:, 256:]))], axis=1)
        last8 = cse[CHUNK - 1:CHUNK, :]
        ssd.append(dict(yd=yd, cm=cm, ecs=jnp.exp(cse), dec=jnp.exp(last8),
                        upd=_dot_tn(bm, xdt * jnp.exp(last8 - cse)) * gmask))

    emit_ffn(1)

    lru = []
    for si, rs in items:
        a_c = la_scr[si, rs, :]
        b_c = lb_scr[si, rs, :]
        s = 1
        while s < SUBLANES:
            keep = pos >= s
            a_sh = jnp.where(keep, pltpu.roll(a_c, s, 0), 1.0)
            b_sh = jnp.where(keep, pltpu.roll(b_c, s, 0), 0.0)
            b_c = a_c * b_sh + b_c
            a_c = a_c * a_sh
            s *= 2
        while s < CHUNK:
            b_c = jnp.concatenate([b_c[:s], a_c[s:] * b_c[:-s] + b_c[s:]], axis=0)
            a_c = jnp.concatenate([a_c[:s], a_c[s:] * a_c[:-s]], axis=0)
            s *= 2
        lru.append((a_c, b_c))

    lru_h = [lruh_scr[si, 0:1, :] for si in range(streams)]
    s_ssd = [ssds_scr[si] for si in range(streams)]
    s_dn = [dns_scr[si] for si in range(streams)]
    for c in range(nch):
        for si in range(streams):
            i = si * nch + c
            rs = items[i][1]
            hseq = lru[i][0] * lru_h[si] + lru[i][1]
            hl_scr[si, rs, :] = hseq
            lru_h[si] = hseq[CHUNK - 1:CHUNK, :]

            sd = ssd[i]
            ys_scr[si, rs, :] = sd["yd"] + _dot(sd["cm"], s_ssd[si]) * sd["ecs"]
            s_ssd[si] = s_ssd[si] * sd["dec"] + sd["upd"]

            d = dn[i]
            ws = _dot(jnp.concatenate([uws[i][:, 256:], d["qe"]], axis=0), s_dn[si])
            v_new = uws[i][:, :256] - ws[:CHUNK]
            od_scr[si, rs, :] = ws[CHUNK:] + _dot(d["attn"], bdiag4(v_new))
            s_dn[si] = s_dn[si] * d["dec"] + _dot_tn(d["k_dec"], v_new) * bd4
        if c < nch - 1:
            emit_ffn(1)

    for si in range(streams):
        lruh_scr[si, 0:1, :] = lru_h[si]
        ssds_scr[si] = s_ssd[si]
        dns_scr[si] = s_dn[si]

    ssdv = ssdv_ref[...]
    gw = SSD_HPG * SSD_HEADDIM
    for si in range(streams):
        proj = proj_scr.at[si]
        o_lru = hl_scr[si] * _gelu_tanh(proj[:, COL_LRU_Y - CONV_W:COL_SSD_Z - CONV_W])
        y = ys_scr[si] + ssdv[0:1] * xs_scr[si]
        y = y * _silu(proj[:, COL_SSD_Z - CONV_W:COL_DN_GATE - CONV_W])
        o_ssd = []
        for g in range(SSD_GROUPS):
            yg = y[:, g * gw:(g + 1) * gw]
            ms = jnp.mean(yg * yg, axis=-1, keepdims=True)
            o_ssd.append(yg * lax.rsqrt(ms + RMS_EPS) * ssdv[1:2, g * gw:(g + 1) * gw])
        od = od_scr[si]
        ms = _segsum_lanes(od * od, ones_bd) * (1.0 / DN_DV)
        o_dn = od * lax.rsqrt(ms + RMS_EPS) * dnv_ref[0:1] * _silu(proj[:, COL_DN_GATE - CONV_W:COL_SMALL - CONV_W])
        mix = jnp.concatenate([o_lru] + o_ssd + [o_dn], axis=1)
        y2 = alpha * h_ref[si] + _dot(mix, wout_ref[...])
        m_out = _layer_norm(y2, lng_ref[...], lnb_ref[...])
        mid_w[...] = m_out
        midb_w[...] = m_out.astype(BF16)
    emit_ffn(len(ffn_units))


def _mixer_scratch(rows, streams):
    rf = lambda *s: pltpu.VMEM((streams,) + s, F32)
    return [
        rf(rows, COL_SMALL - CONV_W), rf(rows, LANES),
        rf(rows + SUBLANES, LRU_WIDTH), rf(rows + SUBLANES, SSD_CONV_DIM), rf(rows + SUBLANES, DN_CONV_DIM),
        rf(rows, 256), rf(rows, 256),
        rf(rows, 256), rf(rows, 256), rf(rows, 256), rf(rows, 256), rf(rows, 256),
        rf(rows, SSD_INNER), rf(rows, SSD_INNER), rf(rows, SSD_INNER),
        rf(rows, LANES), rf(rows, LANES),
        rf(rows, 256), rf(rows, SSD_INNER), rf(rows, 256),
        rf(SUBLANES, 256), rf(LANES, SSD_INNER), rf(256, 256),
    ]


def _mixer_ffn(h3d, p, wg, wu, wd, ln_g, ln_b, layer, *, alpha, rows):
    b, lp, d = h3d.shape
    nb = lp // rows
    nblk = b * nb
    d_ff = wg.shape[-1]
    one = (layer,)
    param_specs = [
        _resident((d, PROJ_W), one), _resident((SUBLANES, CONV_W), one), _resident((LRU_WIDTH, 2 * LRU_WIDTH), one),
        _resident((1, 2 * LRU_WIDTH), one), _resident((1, LRU_WIDTH), one), _resident((SUBLANES, LANES), one),
        _resident((SUBLANES, SSD_INNER), one), _resident((SUBLANES, 256), one), _resident((256, 256)),
        _resident((d, d), one), _resident((1, d), (layer, 1)), _resident((1, d), (layer, 1)),
        _resident((d, d_ff), (layer, 1)), _resident((d, d_ff), (layer, 1)), _resident((d_ff, d), (layer, 1)),
        _resident((1, d), (layer, 2)), _resident((1, d), (layer, 2)),
    ]
    param_args = (p["win"], p["cwb"], p["wgate"], p["gateb"], p["lam"], p["slabv"], p["ssdv"], p["dnv"],
                  p["ones_bd"], p["wout"], ln_g, ln_b, wg, wu, wd, ln_g, ln_b)
    f32_row_words = GATE_W + CONV_W + 8 * 256 + 4 * SSD_INNER + 2 * LANES
    scratch = (rows + SUBLANES) * f32_row_words * 4 + 2 * rows * d * 4 + rows * d_ff * 2
    blocks = 2 * 2 * rows * d * 4
    weights = (d * PROJ_W + d * d + 256 * 512 + 256 * 256 + 3 * d * d_ff) * 2
    vmem = min(scratch + blocks + weights + (12 << 20), V7X_VMEM_BYTES - (4 << 20))

    def in_block(s):
        m = jnp.minimum(s, nblk - 1)
        return (m // nb, m % nb, 0)

    def out_block(s):
        m = jnp.maximum(s - 1, 0)
        return (m // nb, m % nb, 0)

    return pl.pallas_call(
        functools.partial(_mixer_ffn_kernel, alpha=alpha, rows=rows, nb=nb),
        grid=(nblk + 1,),
        in_specs=[pl.BlockSpec((1, rows, d), in_block)] + param_specs,
        out_specs=pl.BlockSpec((1, rows, d), out_block),
        out_shape=jax.ShapeDtypeStruct((b, lp, d), F32),
        scratch_shapes=_mixer_scratch(rows, 1) + [
            pltpu.VMEM((2, rows, d), F32), pltpu.VMEM((2, rows, d), BF16), pltpu.VMEM((rows, d_ff), BF16)],
        compiler_params=pltpu.CompilerParams(dimension_semantics=("arbitrary",), vmem_limit_bytes=vmem),
        name="mixer_ffn",
    )(h3d, *param_args)


def _mixer_params(w_in, lru_conv_w, lru_conv_b, lru_w_a, lru_b_a, lru_w_x, lru_b_x, lru_lambda,
                  ssd_conv_w, ssd_conv_b, ssd_dt_bias, ssd_a_log, ssd_d, ssd_norm_w,
                  dn_conv_w, dn_a_log, dn_dt_bias, dn_norm_w, w_out):
    depth, d, _ = w_in.shape
    o = 0
    parts = {}
    for name, size in (("lru_u", LRU_WIDTH), ("lru_y", LRU_WIDTH), ("ssd_z", SSD_INNER), ("ssd_xbc", SSD_CONV_DIM),
                       ("ssd_dt", SSD_HEADS), ("dn_qkv", DN_CONV_DIM), ("dn_gate", DN_HEADS * DN_DV),
                       ("dn_beta", DN_HEADS), ("dn_alpha", DN_HEADS)):
        parts[name] = w_in[:, :, o:o + size]
        o += size
    zeros = lambda *s: jnp.zeros((depth,) + s, F32)
    win = jnp.concatenate([parts["lru_u"], parts["ssd_xbc"], parts["dn_qkv"], parts["lru_y"], parts["ssd_z"],
                           parts["dn_gate"], parts["ssd_dt"], parts["dn_beta"], parts["dn_alpha"],
                           zeros(d, PROJ_W - COL_SMALL - SSD_HEADS - 2 * DN_HEADS)], axis=2).astype(BF16)
    conv_w = jnp.concatenate([lru_conv_w, ssd_conv_w, dn_conv_w], axis=2)
    conv_b = jnp.concatenate([lru_conv_b, ssd_conv_b, zeros(DN_CONV_DIM)], axis=1)
    cwb = jnp.concatenate([conv_w, conv_b[:, None], zeros(SUBLANES - CONV_K - 1, CONV_W)], axis=1)
    eye = jnp.eye(LRU_HEADS, dtype=F32)
    bdiag = lambda w: jnp.einsum("lhij,hg->lhigj", w, eye).reshape(depth, LRU_WIDTH, LRU_WIDTH)
    wgate = jnp.concatenate([bdiag(lru_w_a), bdiag(lru_w_x)], axis=2).astype(BF16)
    gap = zeros(SM_ALPHA - SM_DT - SSD_HEADS)
    tail = zeros(LANES - SM_ALPHA - DN_HEADS)
    used = jnp.concatenate([jnp.ones((depth, SSD_HEADS), F32), gap, jnp.ones((depth, DN_HEADS), F32), tail], axis=1)
    slab = jnp.stack([jnp.concatenate([ssd_dt_bias, gap, dn_dt_bias, tail], axis=1),
                      jnp.concatenate([ssd_a_log, gap, dn_a_log, tail], axis=1), used]
                     + [zeros(LANES)] * (SUBLANES - 3), axis=1)
    ssdv = jnp.stack([jnp.repeat(ssd_d, SSD_HEADDIM, axis=1), ssd_norm_w] + [zeros(SSD_INNER)] * (SUBLANES - 2),
                     axis=1)
    dnv = jnp.stack([jnp.tile(dn_norm_w, (1, DN_HEADS))] + [zeros(DN_HEADS * DN_DV)] * (SUBLANES - 1), axis=1)
    seg = jnp.arange(256) // 64
    return dict(
        win=win, cwb=cwb, wgate=wgate,
        gateb=jnp.concatenate([lru_b_a, lru_b_x], axis=1)[:, None], lam=lru_lambda[:, None],
        slabv=slab, ssdv=ssdv, dnv=dnv, ones_bd=(seg[:, None] == seg[None, :]).astype(BF16),
        wout=w_out.astype(BF16),
    )


def _plan(batch, length):
    rows = min((320, 256, 128, 64), key=lambda r: (-(-length // r) * r, -r))
    lp = -(-length // rows) * rows
    tm = next(t for t in (1040, 512, 256, 128, 64, 32, 16, 8) if (batch * lp) % t == 0)
    return rows, lp, tm


def _forward(x, meta, ln_g, ln_b, ffn_w_gate, ffn_w_up, ffn_w_down, w_in, lru_conv_w, lru_conv_b, lru_w_a, lru_b_a,
             lru_w_x, lru_b_x, lru_lambda, ssd_conv_w, ssd_conv_b, ssd_dt_bias, ssd_a_log, ssd_d, ssd_norm_w,
             dn_conv_w, dn_a_log, dn_dt_bias, dn_norm_w, w_out, plan=None):
    bsz, seq, d = x.shape
    depth = ln_g.shape[0]
    alpha = float((2 * depth) ** 0.25)
    length = N_META + seq
    rows, lp, tm = plan or _plan(bsz, length)
    h = jnp.concatenate([jnp.broadcast_to(meta.astype(x.dtype)[None], (bsz, N_META, d)), x,
                         jnp.zeros((bsz, lp - length, d), x.dtype)], axis=1)
    wg = ffn_w_gate.astype(BF16)
    wu = ffn_w_up.astype(BF16)
    wd = ffn_w_down.astype(BF16)
    lng = ln_g[:, :, None, :]
    lnb = ln_b[:, :, None, :]
    mp = _mixer_params(w_in, lru_conv_w, lru_conv_b, lru_w_a, lru_b_a, lru_w_x, lru_b_x, lru_lambda,
                       ssd_conv_w, ssd_conv_b, ssd_dt_bias, ssd_a_log, ssd_d, ssd_norm_w,
                       dn_conv_w, dn_a_log, dn_dt_bias, dn_norm_w, w_out)
    for l in range(depth):
        h = _ffn_ln(h.reshape(bsz * lp, d), wg, wu, wd, lng, lnb, l, 0, alpha=alpha, tm=tm).reshape(bsz, lp, d)
        h = _mixer_ffn(h, mp, wg, wu, wd, lng, lnb, l, alpha=alpha, rows=rows)
    return h[:, N_META:length]


def kernel(x, meta, ln_g, ln_b, ffn_w_gate, ffn_w_up, ffn_w_down, w_in, lru_conv_w, lru_conv_b, lru_w_a, lru_b_a,
           lru_w_x, lru_b_x, lru_lambda, ssd_conv_w, ssd_conv_b, ssd_dt_bias, ssd_a_log, ssd_d, ssd_norm_w,
           dn_conv_w, dn_a_log, dn_dt_bias, dn_norm_w, w_out):
    return _forward(x, meta, ln_g, ln_b, ffn_w_gate, ffn_w_up, ffn_w_down, w_in, lru_conv_w, lru_conv_b, lru_w_a,
                    lru_b_a, lru_w_x, lru_b_x, lru_lambda, ssd_conv_w, ssd_conv_b, ssd_dt_bias, ssd_a_log, ssd_d,
                    ssd_norm_w, dn_conv_w, dn_a_log, dn_dt_bias, dn_norm_w, w_out)
```
